```python
import math
import jax, jax.numpy as jnp
from jax import lax
import numpy as np

D_MODEL = 4096
BATCH = 4
SEQ = 2048
DEPTH = 2
DEC_BATCH = 128
DEC_SEQ = 4
PAST_LEN = 16384
PAGE_SIZE = 128

N_MIXERS = 2
N_ATTN_LAYERS = (DEPTH + 1) // 2
N_CONV_LAYERS = DEPTH // 2
N_HEADS = 32
Q_LORA = 1024
KV_LORA = 512
NOPE_DIM = 128
ROPE_DIM = 64
V_DIM = 128
QK_DIM = NOPE_DIM + ROPE_DIM
SM_SCALE = 1.0 / math.sqrt(QK_DIM)
ROPE_THETA = 10000.0
CONV_W = 31
D_FF = -(-8 * D_MODEL // (3 * 256)) * 256
PLE_DIM = 256
Q_BLOCK = 128
EPS = 1e-6

kernel_name = "mla_conformer_conv_hybrid_step"


def _rmsnorm(x, g):
    xf = x.astype(jnp.float32)
    y = xf * lax.rsqrt(jnp.mean(xf * xf, axis=-1, keepdims=True) + EPS)
    return y.astype(x.dtype) * g


def _layernorm(x, g, b):
    xf = x.astype(jnp.float32)
    xc = xf - jnp.mean(xf, axis=-1, keepdims=True)
    y = xc * lax.rsqrt(jnp.mean(xc * xc, axis=-1, keepdims=True) + EPS)
    return y.astype(x.dtype) * g + b


def _rope(x, pos):
    half = ROPE_DIM // 2
    inv_freq = ROPE_THETA ** (-jnp.arange(half, dtype=jnp.float32) / half)
    ang = pos.astype(jnp.float32)[:, None] * inv_freq
    ang = ang.reshape((ang.shape[0],) + (1,) * (x.ndim - 3) + (half,))
    cos, sin = jnp.cos(ang), jnp.sin(ang)
    xf = x.astype(jnp.float32)
    x1, x2 = xf[..., :half], xf[..., half:]
    return jnp.concatenate([x1 * cos - x2 * sin, x2 * cos + x1 * sin], axis=-1).astype(x.dtype)


def _mla_project(h, pos, w_dq, g_q, w_uq, w_dkv, g_kv, w_uk):
    B, T, _ = h.shape
    c_q = _rmsnorm(h @ w_dq, g_q)
    q = (c_q @ w_uq).reshape(B, T, N_HEADS, QK_DIM)
    q_pe = _rope(q[..., NOPE_DIM:], pos)
    q_lat = jnp.einsum('bthn,chn->bthc', q[..., :NOPE_DIM], w_uk)
    kv = h @ w_dkv
    c_kv = _rmsnorm(kv[..., :KV_LORA], g_kv)
    k_pe = _rope(kv[..., KV_LORA:], pos)
    return q_lat, q_pe, c_kv, k_pe


def _mla_attend(q_lat, q_pe, ckv, kpe, mask):
    s = jnp.einsum('bthc,bsc->bhts', q_lat, ckv) + jnp.einsum('bthr,bsr->bhts', q_pe, kpe)
    s = jnp.where(mask, s.astype(jnp.float32) * SM_SCALE, jnp.finfo(jnp.float32).min)
    p = jax.nn.softmax(s, axis=-1).astype(ckv.dtype)
    return jnp.einsum('bhts,bsc->bthc', p, ckv)


def _mla_prompt_attend(q_lat, q_pe, ckv, kpe):
    B, S = q_lat.shape[:2]
    nblk = S // Q_BLOCK
    qb = q_lat.reshape(B, nblk, Q_BLOCK, N_HEADS, KV_LORA).swapaxes(0, 1)
    pb = q_pe.reshape(B, nblk, Q_BLOCK, N_HEADS, ROPE_DIM).swapaxes(0, 1)
    starts = jnp.arange(nblk, dtype=jnp.int32) * Q_BLOCK
    kpos = jnp.arange(S, dtype=jnp.int32)

    def blk(args):
        qc, pc, st = args
        qpos = st + jnp.arange(Q_BLOCK, dtype=jnp.int32)
        return _mla_attend(qc, pc, ckv, kpe, kpos[None, :] <= qpos[:, None])

    o = lax.map(blk, (qb, pb, starts))
    return o.swapaxes(0, 1).reshape(B, S, N_HEADS, KV_LORA)


def _mla_out(o_lat, w_uv, w_o):
    B, T = o_lat.shape[:2]
    o = jnp.einsum('bthc,chv->bthv', o_lat, w_uv).reshape(B, T, N_HEADS * V_DIM)
    return o @ w_o


def _conv_module(h, prev, w_pw1, b_pw1, w_dw, b_dw, g_ln, b_ln, w_pw2, b_pw2):
    a = h @ w_pw1 + b_pw1
    u = a[..., :D_MODEL] * jax.nn.sigmoid(a[..., D_MODEL:])
    full = jnp.concatenate([prev.astype(u.dtype), u], axis=1)
    y = lax.conv_general_dilated(full, w_dw[:, None, :].astype(full.dtype), (1,), 'VALID',
                                 dimension_numbers=('NWC', 'WIO', 'NWC'),
                                 feature_group_count=D_MODEL) + b_dw
    y = _layernorm(y, g_ln, b_ln)
    y = y * jax.nn.sigmoid(y)
    return y @ w_pw2 + b_pw2, full[:, -(CONV_W - 1):]


def _ffn(h, g, wg, wu, wd):
    z = _rmsnorm(h, g)
    return h + (jax.nn.silu(z @ wg) * (z @ wu)) @ wd


def _ple(h, p_i, g, w_gate, w_up):
    return h + jax.nn.sigmoid(_rmsnorm(h, g) @ w_gate) * (p_i @ w_up)


def setup_inputs(seed: int = 0) -> dict:
    key = jax.random.key(seed)
    ks = iter(jax.random.split(key, 48))
    f32 = jnp.float32

    def nrm(shape, scale=1.0):
        return jax.random.normal(next(ks), shape, f32) * scale

    def gain(shape):
        return 1.0 + 0.01 * jax.random.normal(next(ks), shape, f32)

    n_pages = PAST_LEN // PAGE_SIZE
    n_used = DEC_BATCH * n_pages
    n_pool = n_used + max(1, n_used // 4)
    LA, LC, L = N_ATTN_LAYERS, N_CONV_LAYERS, DEPTH
    d = {}
    d['x_prompt'] = nrm((BATCH, SEQ, D_MODEL))
    d['x_sample'] = nrm((DEC_BATCH, DEC_SEQ, D_MODEL))
    d['cache_ckv'] = nrm((LA, n_pool, PAGE_SIZE, KV_LORA))
    d['cache_kpe'] = nrm((LA, n_pool, PAGE_SIZE, ROPE_DIM))
    d['state_conv'] = nrm((LC, DEC_BATCH, CONV_W - 1, D_MODEL), 0.5)
    d['page_table'] = jax.random.permutation(next(ks), n_pool)[:n_used].reshape(DEC_BATCH, n_pages).astype(jnp.int32)
    d['p_prompt'] = nrm((L, BATCH, SEQ, PLE_DIM))
    d['p_sample'] = nrm((L, DEC_BATCH, DEC_SEQ, PLE_DIM))
    d['g_attn_norm'] = gain((LA, D_MODEL))
    d['w_dq'] = nrm((LA, D_MODEL, Q_LORA), D_MODEL ** -0.5)
    d['g_q'] = gain((LA, Q_LORA))
    d['w_uq'] = nrm((LA, Q_LORA, N_HEADS * QK_DIM), Q_LORA ** -0.5)
    d['w_dkv'] = nrm((LA, D_MODEL, KV_LORA + ROPE_DIM), D_MODEL ** -0.5)
    d['g_kv'] = gain((LA, KV_LORA))
    d['w_uk'] = nrm((LA, KV_LORA, N_HEADS, NOPE_DIM), KV_LORA ** -0.5)
    d['w_uv'] = nrm((LA, KV_LORA, N_HEADS, V_DIM), KV_LORA ** -0.5)
    d['w_o'] = nrm((LA, N_HEADS * V_DIM, D_MODEL), (N_HEADS * V_DIM) ** -0.5)
    d['g_conv_norm'] = gain((LC, D_MODEL))
    d['w_pw1'] = nrm((LC, D_MODEL, 2 * D_MODEL), D_MODEL ** -0.5)
    d['b_pw1'] = nrm((LC, 2 * D_MODEL), 0.01)
    d['w_dw'] = nrm((LC, CONV_W, D_MODEL), CONV_W ** -0.5)
    d['b_dw'] = nrm((LC, D_MODEL), 0.01)
    d['g_conv_ln'] = gain((LC, D_MODEL))
    d['b_conv_ln'] = nrm((LC, D_MODEL), 0.01)
    d['w_pw2'] = nrm((LC, D_MODEL, D_MODEL), D_MODEL ** -0.5)
    d['b_pw2'] = nrm((LC, D_MODEL), 0.01)
    d['g_ffn_norm'] = gain((L, D_MODEL))
    d['w_ffn_gate'] = nrm((L, D_MODEL, D_FF), D_MODEL ** -0.5)
    d['w_ffn_up'] = nrm((L, D_MODEL, D_FF), D_MODEL ** -0.5)
    d['w_ffn_down'] = nrm((L, D_FF, D_MODEL), D_FF ** -0.5)
    d['g_ple_norm'] = gain((L, D_MODEL))
    d['w_ple_gate'] = nrm((L, D_MODEL, D_MODEL), D_MODEL ** -0.5)
    d['w_ple_up'] = nrm((L, PLE_DIM, D_MODEL), PLE_DIM ** -0.5)
    d['g_final'] = gain((D_MODEL,))
    return d


def reference(x_prompt, x_sample, cache_ckv, cache_kpe, state_conv, page_table, p_prompt, p_sample,
              g_attn_norm, w_dq, g_q, w_uq, w_dkv, g_kv, w_uk, w_uv, w_o,
              g_conv_norm, w_pw1, b_pw1, w_dw, b_dw, g_conv_ln, b_conv_ln, w_pw2, b_pw2,
              g_ffn_norm, w_ffn_gate, w_ffn_up, w_ffn_down,
              g_ple_norm, w_ple_gate, w_ple_up, g_final):
    B, S, _ = x_prompt.shape
    DB, T, _ = x_sample.shape
    past = page_table.shape[1] * cache_ckv.shape[2]
    pos_p = jnp.arange(S, dtype=jnp.int32)
    pos_s = past + jnp.arange(T, dtype=jnp.int32)
    mask_s = jnp.concatenate([jnp.ones((T, past), bool), jnp.tril(jnp.ones((T, T), bool))], axis=1)

    hp, hs = x_prompt, x_sample
    ckv_p, kpe_p, conv_p, ckv_s, kpe_s, conv_s = [], [], [], [], [], []
    for i in range(DEPTH):
        j = i // N_MIXERS
        if i % N_MIXERS == 0:
            proj = (w_dq[j], g_q[j], w_uq[j], w_dkv[j], g_kv[j], w_uk[j])
            ql, qp, ck, kp = _mla_project(_rmsnorm(hp, g_attn_norm[j]), pos_p, *proj)
            hp = hp + _mla_out(_mla_prompt_attend(ql, qp, ck, kp), w_uv[j], w_o[j])
            ckv_p.append(ck)
            kpe_p.append(kp)
            ql, qp, ck, kp = _mla_project(_rmsnorm(hs, g_attn_norm[j]), pos_s, *proj)
            past_ck = cache_ckv[j][page_table].reshape(DB, past, KV_LORA)
            past_kp = cache_kpe[j][page_table].reshape(DB, past, ROPE_DIM)
            keys_ck = jnp.concatenate([past_ck.astype(ck.dtype), ck], axis=1)
            keys_kp = jnp.concatenate([past_kp.astype(kp.dtype), kp], axis=1)
            hs = hs + _mla_out(_mla_attend(ql, qp, keys_ck, keys_kp, mask_s), w_uv[j], w_o[j])
            ckv_s.append(ck)
            kpe_s.append(kp)
        else:
            cw = (w_pw1[j], b_pw1[j], w_dw[j], b_dw[j], g_conv_ln[j], b_conv_ln[j], w_pw2[j], b_pw2[j])
            zeros = jnp.zeros((B, CONV_W - 1, D_MODEL), hp.dtype)
            mp, np_ = _conv_module(_rmsnorm(hp, g_conv_norm[j]), zeros, *cw)
            hp = hp + mp
            conv_p.append(np_)
            ms, ns = _conv_module(_rmsnorm(hs, g_conv_norm[j]), state_conv[j], *cw)
            hs = hs + ms
            conv_s.append(ns)
        fw = (g_ffn_norm[i], w_ffn_gate[i], w_ffn_up[i], w_ffn_down[i])
        hp = _ffn(hp, *fw)
        hs = _ffn(hs, *fw)
        hp = _ple(hp, p_prompt[i], g_ple_norm[i], w_ple_gate[i], w_ple_up[i])
        hs = _ple(hs, p_sample[i], g_ple_norm[i], w_ple_gate[i], w_ple_up[i])

    y_prompt = _rmsnorm(hp, g_final)
    y_sample = _rmsnorm(hs, g_final)
    return (y_prompt, y_sample,
            jnp.stack(ckv_p), jnp.stack(kpe_p), jnp.stack(conv_p),
            jnp.stack(ckv_s), jnp.stack(kpe_s), jnp.stack(conv_s))
```

```python
import functools
import math

import jax
import jax.numpy as jnp
from jax import lax
from jax.experimental import pallas as pl
from jax.experimental.pallas import tpu as pltpu

D_MODEL = 4096
N_HEADS = 32
Q_LORA = 1024
KV_LORA = 512
NOPE_DIM = 128
ROPE_DIM = 64
V_DIM = 128
QK_DIM = NOPE_DIM + ROPE_DIM
SM_SCALE = 1.0 / math.sqrt(QK_DIM)
ROPE_THETA = 10000.0
CONV_W = 31
EPS = 1e-6

LANES = 128
ROPE_PAD = LANES
QK_PAD = KV_LORA + ROPE_PAD
NEG_BIG = -1e30
VMEM_LIMIT = 56 * 1024 * 1024

F32 = jnp.float32
BF16 = jnp.bfloat16


def _cparams(n_axes):
    return pltpu.CompilerParams(dimension_semantics=("arbitrary",) * n_axes,
                                vmem_limit_bytes=VMEM_LIMIT)


def _dot(a, b):
    return jnp.dot(a, b, preferred_element_type=F32)


def _dot_nt(a, b):
    return lax.dot_general(a, b, (((1,), (1,)), ((), ())), preferred_element_type=F32)


def _rmsnorm_kernel(x_ref, g_ref, o_ref):
    x = x_ref[...]
    y = x * lax.rsqrt(jnp.mean(x * x, axis=-1, keepdims=True) + EPS)
    o_ref[...] = (y * g_ref[...]).astype(o_ref.dtype)


def _rmsnorm(x, g, out_dtype, tm=256):
    m, d = x.shape
    return pl.pallas_call(
        _rmsnorm_kernel,
        out_shape=jax.ShapeDtypeStruct((m, d), out_dtype),
        grid=(m // tm,),
        in_specs=[pl.BlockSpec((tm, d), lambda i: (i, 0)),
                  pl.BlockSpec((1, d), lambda i: (0, 0))],
        out_specs=pl.BlockSpec((tm, d), lambda i: (i, 0)),
        compiler_params=_cparams(1),
        name="rmsnorm",
    )(x, g.reshape(1, d))


def _matmul(xs, wspecs, extras, epilogue, out_dtypes, *, n_cols, tm, tn, name):
    m = xs[0].shape[0]
    nx, nw, ne, no = len(xs), len(wspecs), len(extras), len(out_dtypes)
    in_specs = [pl.BlockSpec((tm, x.shape[1]), lambda i, j: (i, 0)) for x in xs]
    for _, w, off in wspecs:
        in_specs.append(pl.BlockSpec((w.shape[0], tn), lambda i, j, off=off: (0, j + off)))
    for _, kind in extras:
        if kind == "tile":
            in_specs.append(pl.BlockSpec((tm, tn), lambda i, j: (i, j)))
        else:
            in_specs.append(pl.BlockSpec((1, tn), lambda i, j: (0, j)))
    x_index = [xi for xi, _, _ in wspecs]

    def kernel(*refs):
        x_refs, w_refs = refs[:nx], refs[nx:nx + nw]
        e_refs, o_refs = refs[nx + nw:nx + nw + ne], refs[nx + nw + ne:]
        xv = [r[...].astype(BF16) for r in x_refs]
        accs = [_dot(xv[xi], w_ref[...]) for xi, w_ref in zip(x_index, w_refs)]
        outs = epilogue(accs, [e[...] for e in e_refs])
        for o_ref, o in zip(o_refs, outs):
            o_ref[...] = o.astype(o_ref.dtype)

    outs = pl.pallas_call(
        kernel,
        out_shape=[jax.ShapeDtypeStruct((m, n_cols), dt) for dt in out_dtypes],
        grid=(m // tm, n_cols // tn),
        in_specs=in_specs,
        out_specs=[pl.BlockSpec((tm, tn), lambda i, j: (i, j)) for _ in range(no)],
        compiler_params=_cparams(2),
        name=name,
    )(*xs, *[w for _, w, _ in wspecs], *[a for a, _ in extras])
    return outs


def _kv_proj_kernel(z_ref, wdq_ref, wkv_ref, gq_ref, gkv_ref, cos_ref, sin_ref,
                    cq_ref, ckv_ref, kpe_ref, k_ref):
    z = z_ref[...]
    a = _dot(z, wdq_ref[...])
    cq = a * lax.rsqrt(jnp.mean(a * a, axis=-1, keepdims=True) + EPS) * gq_ref[...]
    cq_ref[...] = cq.astype(cq_ref.dtype)
    kv = _dot(z, wkv_ref[...])
    c = kv[:, :KV_LORA]
    ckv = c * lax.rsqrt(jnp.mean(c * c, axis=-1, keepdims=True) + EPS) * gkv_ref[...]
    kpe = (kv[:, KV_LORA:KV_LORA + ROPE_PAD] * cos_ref[...]
           + kv[:, KV_LORA + ROPE_PAD:] * sin_ref[...])
    ckv_ref[...] = ckv
    kpe_ref[...] = kpe[:, :ROPE_DIM]
    k_ref[:, :KV_LORA] = ckv.astype(k_ref.dtype)
    k_ref[:, KV_LORA:] = kpe.astype(k_ref.dtype)


def _kv_proj(z, wdq, wkv, gq, gkv, cos, sin, tm=512):
    m = z.shape[0]
    row = lambda i: (i, 0)
    fixed = lambda i: (0, 0)
    return pl.pallas_call(
        _kv_proj_kernel,
        out_shape=[jax.ShapeDtypeStruct((m, Q_LORA), BF16),
                   jax.ShapeDtypeStruct((m, KV_LORA), F32),
                   jax.ShapeDtypeStruct((m, ROPE_DIM), F32),
                   jax.ShapeDtypeStruct((m, QK_PAD), BF16)],
        grid=(m // tm,),
        in_specs=[pl.BlockSpec((tm, D_MODEL), row),
                  pl.BlockSpec(wdq.shape, fixed),
                  pl.BlockSpec(wkv.shape, fixed),
                  pl.BlockSpec((1, Q_LORA), fixed),
                  pl.BlockSpec((1, KV_LORA), fixed),
                  pl.BlockSpec((tm, ROPE_PAD), row),
                  pl.BlockSpec((tm, ROPE_PAD), row)],
        out_specs=[pl.BlockSpec((tm, Q_LORA), row),
                   pl.BlockSpec((tm, KV_LORA), row),
                   pl.BlockSpec((tm, ROPE_DIM), row),
                   pl.BlockSpec((tm, QK_PAD), row)],
        compiler_params=_cparams(1),
        name="kv_proj",
    )(z, wdq, wkv, gq.reshape(1, -1), gkv.reshape(1, -1), cos, sin)


Q_HEADS_PER_STEP = 4


def _q_proj_kernel(cq_ref, wn_ref, wp_ref, wps_ref, wuk_ref, cos_ref, sin_ref, q_ref):
    cq = cq_ref[...]
    qn = _dot(cq, wn_ref[...]).astype(BF16)
    qp = _dot(cq, wp_ref[...])
    qps = _dot(cq, wps_ref[...])
    cos, sin = cos_ref[...], sin_ref[...]
    for h in range(Q_HEADS_PER_STEP):
        ql = _dot(qn[:, h * NOPE_DIM:(h + 1) * NOPE_DIM], wuk_ref[h])
        qr = (qp[:, h * ROPE_PAD:(h + 1) * ROPE_PAD] * cos
              + qps[:, h * ROPE_PAD:(h + 1) * ROPE_PAD] * sin)
        q_ref[:, h * QK_PAD:h * QK_PAD + KV_LORA] = ql.astype(q_ref.dtype)
        q_ref[:, h * QK_PAD + KV_LORA:(h + 1) * QK_PAD] = qr.astype(q_ref.dtype)


def _q_proj(cq, wn, wp, wps, wuk_t, cos, sin, tm):
    m = cq.shape[0]
    hq = Q_HEADS_PER_STEP
    return pl.pallas_call(
        _q_proj_kernel,
        out_shape=jax.ShapeDtypeStruct((m, N_HEADS * QK_PAD), BF16),
        grid=(m // tm, N_HEADS // hq),
        in_specs=[pl.BlockSpec((tm, Q_LORA), lambda i, j: (i, 0)),
                  pl.BlockSpec((Q_LORA, hq * NOPE_DIM), lambda i, j: (0, j)),
                  pl.BlockSpec((Q_LORA, hq * ROPE_PAD), lambda i, j: (0, j)),
                  pl.BlockSpec((Q_LORA, hq * ROPE_PAD), lambda i, j: (0, j)),
                  pl.BlockSpec((hq, NOPE_DIM, KV_LORA), lambda i, j: (j, 0, 0)),
                  pl.BlockSpec((tm, ROPE_PAD), lambda i, j: (i, 0)),
                  pl.BlockSpec((tm, ROPE_PAD), lambda i, j: (i, 0))],
        out_specs=pl.BlockSpec((tm, hq * QK_PAD), lambda i, j: (i, j)),
        compiler_params=_cparams(2),
        name="q_proj",
    )(cq, wn, wp, wps, wuk_t, cos, sin)


def _softmax_update(s, v, m_sc, l_sc, acc_sc):
    m_prev = m_sc[...]
    m_new = jnp.maximum(m_prev, jnp.max(s, axis=1, keepdims=True))
    alpha = jnp.exp(m_prev - m_new)
    p = jnp.exp(s - m_new)
    l_sc[...] = alpha * l_sc[...] + jnp.sum(p, axis=1, keepdims=True)
    acc_sc[...] = alpha * acc_sc[...] + _dot(p.astype(BF16), v)
    m_sc[...] = m_new


PROMPT_TQ = 256
PROMPT_HG = 4


def _prompt_attn_kernel(q_ref, k_ref, wuv_ref, o_ref, m_sc, l_sc, acc_sc):
    tq, hg = PROMPT_TQ, PROMPT_HG
    i = pl.program_id(1)
    q = jnp.concatenate([q_ref[:, h * QK_PAD:(h + 1) * QK_PAD] for h in range(hg)], axis=0)
    m_sc[...] = jnp.full(m_sc.shape, -jnp.inf, F32)
    l_sc[...] = jnp.zeros(l_sc.shape, F32)
    acc_sc[...] = jnp.zeros(acc_sc.shape, F32)

    def full_tile(kt, carry):
        k = k_ref[pl.ds(pl.multiple_of(kt * tq, tq), tq), :]
        s = _dot_nt(q, k) * SM_SCALE
        _softmax_update(s, k[:, :KV_LORA], m_sc, l_sc, acc_sc)
        return carry

    lax.fori_loop(0, i, full_tile, 0)

    k = k_ref[pl.ds(pl.multiple_of(i * tq, tq), tq), :]
    s = _dot_nt(q, k) * SM_SCALE
    tok = lax.broadcasted_iota(jnp.int32, s.shape, 0) & (tq - 1)
    key = lax.broadcasted_iota(jnp.int32, s.shape, 1)
    s = jnp.where(key <= tok, s, NEG_BIG)
    _softmax_update(s, k[:, :KV_LORA], m_sc, l_sc, acc_sc)

    o_lat = (acc_sc[...] / l_sc[...]).astype(BF16)
    for h in range(hg):
        o_ref[:, h * V_DIM:(h + 1) * V_DIM] = _dot(
            o_lat[h * tq:(h + 1) * tq, :], wuv_ref[h]).astype(o_ref.dtype)


def _prompt_attn(q, k, wuv_t, batch, seq):
    tq, hg = PROMPT_TQ, PROMPT_HG
    nq = seq // tq
    rows = tq * hg
    return pl.pallas_call(
        _prompt_attn_kernel,
        out_shape=jax.ShapeDtypeStruct((batch * seq, N_HEADS * V_DIM), BF16),
        grid=(batch, nq, N_HEADS // hg),
        in_specs=[pl.BlockSpec((tq, hg * QK_PAD), lambda b, i, g: (b * nq + i, g)),
                  pl.BlockSpec((seq, QK_PAD), lambda b, i, g: (b, 0)),
                  pl.BlockSpec((hg, KV_LORA, V_DIM), lambda b, i, g: (g, 0, 0))],
        out_specs=pl.BlockSpec((tq, hg * V_DIM), lambda b, i, g: (b * nq + i, g)),
        scratch_shapes=[pltpu.VMEM((rows, 1), F32), pltpu.VMEM((rows, 1), F32),
                        pltpu.VMEM((rows, KV_LORA), F32)],
        compiler_params=_cparams(3),
        name="prompt_attn",
    )(q, k, wuv_t)


DEC_PAGES = 16
DEC_NEW_PAD = 16


def _decode_attn_kernel(pt_ref, q_ref, knew_ref, *refs, n_new):
    del pt_ref
    ck_refs, kp_refs = refs[:DEC_PAGES], refs[DEC_PAGES:2 * DEC_PAGES]
    o_ref, m_sc, l_sc, acc_sc = refs[2 * DEC_PAGES:]
    c = pl.program_id(1)

    @pl.when(c == 0)
    def _():
        m_sc[...] = jnp.full(m_sc.shape, -jnp.inf, F32)
        l_sc[...] = jnp.zeros(l_sc.shape, F32)
        acc_sc[...] = jnp.zeros(acc_sc.shape, F32)

    q = q_ref[0]
    ck = jnp.concatenate([r[0, 0].astype(BF16) for r in ck_refs], axis=0)
    kp = jnp.concatenate([r[0, 0].astype(BF16) for r in kp_refs], axis=0)
    s = (_dot_nt(q[:, :KV_LORA], ck) + _dot_nt(q[:, KV_LORA:KV_LORA + ROPE_DIM], kp)) * SM_SCALE
    _softmax_update(s, ck, m_sc, l_sc, acc_sc)

    @pl.when(c == pl.num_programs(1) - 1)
    def _():
        kn = knew_ref[0]
        sn = _dot_nt(q, kn) * SM_SCALE
        tok = lax.broadcasted_iota(jnp.int32, sn.shape, 0) >> (N_HEADS.bit_length() - 1)
        key = lax.broadcasted_iota(jnp.int32, sn.shape, 1)
        sn = jnp.where((key <= tok) & (key < n_new), sn, NEG_BIG)
        _softmax_update(sn, kn[:, :KV_LORA], m_sc, l_sc, acc_sc)
        o_ref[0] = (acc_sc[...] / l_sc[...]).astype(o_ref.dtype)


def _decode_attn(q, knew, cache_ckv, cache_kpe, page_table, layer, n_new):
    db, rows, _ = q.shape
    n_pages = page_table.shape[1]
    page = cache_ckv.shape[2]
    nc = n_pages // DEC_PAGES

    def page_map(p):
        return lambda b, c, pt: (layer, pt[b, c * DEC_PAGES + p], 0, 0)

    in_specs = [pl.BlockSpec((1, rows, QK_PAD), lambda b, c, pt: (b, 0, 0)),
                pl.BlockSpec((1, DEC_NEW_PAD, QK_PAD), lambda b, c, pt: (b, 0, 0))]
    in_specs += [pl.BlockSpec((1, 1, page, KV_LORA), page_map(p)) for p in range(DEC_PAGES)]
    in_specs += [pl.BlockSpec((1, 1, page, ROPE_DIM), page_map(p)) for p in range(DEC_PAGES)]
    return pl.pallas_call(
        functools.partial(_decode_attn_kernel, n_new=n_new),
        out_shape=jax.ShapeDtypeStruct((db, rows, KV_LORA), BF16),
        grid_spec=pltpu.PrefetchScalarGridSpec(
            num_scalar_prefetch=1,
            grid=(db, nc),
            in_specs=in_specs,
            out_specs=pl.BlockSpec((1, rows, KV_LORA), lambda b, c, pt: (b, 0, 0)),
            scratch_shapes=[pltpu.VMEM((rows, 1), F32), pltpu.VMEM((rows, 1), F32),
                            pltpu.VMEM((rows, KV_LORA), F32)]),
        compiler_params=_cparams(2),
        name="decode_attn",
    )(page_table, q, knew, *([cache_ckv] * DEC_PAGES), *([cache_kpe] * DEC_PAGES))


UV_HEADS_PER_STEP = 4


def _uv_proj_kernel(x_ref, w_ref, o_ref):
    for h in range(UV_HEADS_PER_STEP):
        o_ref[:, h * V_DIM:(h + 1) * V_DIM] = _dot(
            x_ref[:, h * KV_LORA:(h + 1) * KV_LORA], w_ref[h]).astype(o_ref.dtype)


def _uv_proj(o_lat, wuv_t, tm):
    m = o_lat.shape[0]
    hu = UV_HEADS_PER_STEP
    return pl.pallas_call(
        _uv_proj_kernel,
        out_shape=jax.ShapeDtypeStruct((m, N_HEADS * V_DIM), BF16),
        grid=(m // tm, N_HEADS // hu),
        in_specs=[pl.BlockSpec((tm, hu * KV_LORA), lambda i, j: (i, j)),
                  pl.BlockSpec((hu, KV_LORA, V_DIM), lambda i, j: (j, 0, 0))],
        out_specs=pl.BlockSpec((tm, hu * V_DIM), lambda i, j: (i, j)),
        compiler_params=_cparams(2),
        name="uv_proj",
    )(o_lat, wuv_t)


CONV_TS = 256
CONV_HALO = 32
CONV_RC = 128


def _ln_silu(y, g, b):
    yc = y - jnp.mean(y, axis=-1, keepdims=True)
    yn = yc * lax.rsqrt(jnp.mean(yc * yc, axis=-1, keepdims=True) + EPS) * g + b
    return yn * jax.nn.sigmoid(yn)


def _conv_prompt_kernel(halo_ref, x_ref, w_ref, bdw_ref, g_ref, b_ref, o_ref, xw_sc, y_sc):
    ts = CONV_TS
    i = pl.program_id(1)
    xw_sc[0:CONV_HALO, :] = jnp.where(i > 0, halo_ref[...], 0.0)
    xw_sc[CONV_HALO:CONV_HALO + ts, :] = x_ref[...]
    base = CONV_HALO - (CONV_W - 1)

    def col_chunk(c, carry):
        col = pl.multiple_of(c * LANES, LANES)
        for r in range(ts // CONV_RC):
            acc = jnp.zeros((CONV_RC, LANES), F32)
            for k in range(CONV_W):
                acc = acc + (w_ref[k:k + 1, pl.ds(col, LANES)]
                             * xw_sc[base + r * CONV_RC + k:base + r * CONV_RC + k + CONV_RC,
                                     pl.ds(col, LANES)])
            y_sc[r * CONV_RC:(r + 1) * CONV_RC, pl.ds(col, LANES)] = acc + bdw_ref[:, pl.ds(col, LANES)]
        return carry

    lax.fori_loop(0, D_MODEL // LANES, col_chunk, 0)
    for r in range(ts // CONV_RC):
        rs = slice(r * CONV_RC, (r + 1) * CONV_RC)
        o_ref[rs, :] = _ln_silu(y_sc[rs, :], g_ref[...], b_ref[...]).astype(o_ref.dtype)


def _conv_prompt(u, w_dw, b_dw, g_ln, b_ln, batch, seq):
    ts = CONV_TS
    nt = seq // ts
    hb = ts // CONV_HALO
    fixed = lambda b, i: (0, 0)
    return pl.pallas_call(
        _conv_prompt_kernel,
        out_shape=jax.ShapeDtypeStruct((batch * seq, D_MODEL), BF16),
        grid=(batch, nt),
        in_specs=[pl.BlockSpec((CONV_HALO, D_MODEL),
                               lambda b, i: (jnp.maximum((b * nt + i) * hb - 1, 0), 0)),
                  pl.BlockSpec((ts, D_MODEL), lambda b, i: (b * nt + i, 0)),
                  pl.BlockSpec((CONV_W, D_MODEL), fixed),
                  pl.BlockSpec((1, D_MODEL), fixed),
                  pl.BlockSpec((1, D_MODEL), fixed),
                  pl.BlockSpec((1, D_MODEL), fixed)],
        out_specs=pl.BlockSpec((ts, D_MODEL), lambda b, i: (b * nt + i, 0)),
        scratch_shapes=[pltpu.VMEM((CONV_HALO + ts, D_MODEL), F32),
                        pltpu.VMEM((ts, D_MODEL), F32)],
        compiler_params=_cparams(2),
        name="conv_prompt",
    )(u, u, w_dw, b_dw.reshape(1, -1), g_ln.reshape(1, -1), b_ln.reshape(1, -1))


CONV_SB = 8


def _conv_sample_kernel(st_ref, u_ref, w_ref, bdw_ref, g_ref, b_ref, o_ref, ns_ref, full_sc):
    n_prev = CONV_W - 1
    t = u_ref.shape[1]

    def one(bi, carry):
        full_sc[0:n_prev, :] = st_ref[bi]
        full_sc[n_prev:n_prev + t, :] = u_ref[bi]
        acc = jnp.zeros((t, D_MODEL), F32)
        for k in range(CONV_W):
            acc = acc + w_ref[k:k + 1, :] * full_sc[k:k + t, :]
        o_ref[bi] = _ln_silu(acc + bdw_ref[...], g_ref[...], b_ref[...]).astype(o_ref.dtype)
        ns_ref[bi] = full_sc[t:t + n_prev, :]
        return carry

    lax.fori_loop(0, CONV_SB, one, 0)


def _conv_sample(state, u, w_dw, b_dw, g_ln, b_ln):
    db, n_prev, d = state.shape
    t = u.shape[1]
    sb = CONV_SB
    fixed = lambda i: (0, 0)
    blk = lambda i: (i, 0, 0)
    return pl.pallas_call(
        _conv_sample_kernel,
        out_shape=[jax.ShapeDtypeStruct((db, t, d), F32),
                   jax.ShapeDtypeStruct((db, n_prev, d), F32)],
        grid=(db // sb,),
        in_specs=[pl.BlockSpec((sb, n_prev, d), blk),
                  pl.BlockSpec((sb, t, d), blk),
                  pl.BlockSpec((CONV_W, d), fixed),
                  pl.BlockSpec((1, d), fixed),
                  pl.BlockSpec((1, d), fixed),
                  pl.BlockSpec((1, d), fixed)],
        out_specs=[pl.BlockSpec((sb, t, d), blk), pl.BlockSpec((sb, n_prev, d), blk)],
        scratch_shapes=[pltpu.VMEM((n_prev + t + 6, d), F32)],
        compiler_params=_cparams(1),
        name="conv_sample",
    )(state, u, w_dw, b_dw.reshape(1, -1), g_ln.reshape(1, -1), b_ln.reshape(1, -1))


def _ffn(h, g, wg, wu, wd, tm):
    z = _rmsnorm(h, g, BF16)
    d_ff = wg.shape[1]
    act, = _matmul([z], [(0, wg, 0), (0, wu, 0)], [],
                   lambda accs, ex: [jax.nn.silu(accs[0]) * accs[1]], [BF16],
                   n_cols=d_ff, tm=tm, tn=256, name="ffn_gate_up")
    out, = _matmul([act], [(0, wd, 0)], [(h, "tile")],
                   lambda accs, ex: [ex[0] + accs[0]], [F32],
                   n_cols=D_MODEL, tm=512, tn=256, name="ffn_down")
    return out


def _ple(h, p, g, w_gate, w_up, tm):
    z = _rmsnorm(h, g, BF16)
    out, = _matmul([z, p], [(0, w_gate, 0), (1, w_up, 0)], [(h, "tile")],
                   lambda accs, ex: [ex[0] + jax.nn.sigmoid(accs[0]) * accs[1]], [F32],
                   n_cols=D_MODEL, tm=tm, tn=512, name="ple")
    return out


def _rope_tables(pos):
    half = ROPE_DIM // 2
    inv_freq = ROPE_THETA ** (-jnp.arange(half, dtype=F32) / half)
    ang = pos.astype(F32)[:, None] * inv_freq
    cos, sin = jnp.cos(ang), jnp.sin(ang)
    pad = jnp.zeros((pos.shape[0], ROPE_PAD - ROPE_DIM), F32)
    return (jnp.concatenate([cos, cos, pad], axis=1),
            jnp.concatenate([-sin, sin, pad], axis=1))


def _swap_halves(w):
    half = ROPE_DIM // 2
    return jnp.concatenate([w[..., half:], w[..., :half]], axis=-1)


def _pad_rope(w):
    return jnp.concatenate([w, jnp.zeros(w.shape[:-1] + (ROPE_PAD - ROPE_DIM,), w.dtype)], axis=-1)


def kernel(x_prompt, x_sample, cache_ckv, cache_kpe, state_conv, page_table, p_prompt, p_sample,
           g_attn_norm, w_dq, g_q, w_uq, w_dkv, g_kv, w_uk, w_uv, w_o,
           g_conv_norm, w_pw1, b_pw1, w_dw, b_dw, g_conv_ln, b_conv_ln, w_pw2, b_pw2,
           g_ffn_norm, w_ffn_gate, w_ffn_up, w_ffn_down,
           g_ple_norm, w_ple_gate, w_ple_up, g_final):
    B, S, _ = x_prompt.shape
    DB, T, _ = x_sample.shape
    depth = g_ffn_norm.shape[0]
    past = page_table.shape[1] * cache_ckv.shape[2]
    MP, MS = B * S, DB * T
    TM_P, TM_S = 1024, MS

    cos_p, sin_p = _rope_tables(jnp.tile(jnp.arange(S, dtype=jnp.int32), B))
    cos_s, sin_s = _rope_tables(jnp.tile(past + jnp.arange(T, dtype=jnp.int32), DB))

    hp = x_prompt.reshape(MP, D_MODEL)
    hs = x_sample.reshape(MS, D_MODEL)
    ckv_p, kpe_p, conv_p, ckv_s, kpe_s, conv_s = [], [], [], [], [], []

    for i in range(depth):
        j = i // 2
        if i % 2 == 0:
            wdq = w_dq[j].astype(BF16)
            w_kpe = w_dkv[j][:, KV_LORA:]
            wkv = jnp.concatenate([w_dkv[j][:, :KV_LORA], _pad_rope(w_kpe),
                                   _pad_rope(_swap_halves(w_kpe))], axis=1).astype(BF16)
            wq3 = w_uq[j].reshape(Q_LORA, N_HEADS, QK_DIM)
            wn = wq3[:, :, :NOPE_DIM].reshape(Q_LORA, N_HEADS * NOPE_DIM).astype(BF16)
            wp = _pad_rope(wq3[:, :, NOPE_DIM:]).reshape(Q_LORA, N_HEADS * ROPE_PAD).astype(BF16)
            wps = _pad_rope(_swap_halves(wq3[:, :, NOPE_DIM:])).reshape(
                Q_LORA, N_HEADS * ROPE_PAD).astype(BF16)
            wuk_t = jnp.transpose(w_uk[j], (1, 2, 0)).astype(BF16)
            wuv_t = jnp.transpose(w_uv[j], (1, 0, 2)).astype(BF16)
            wo = w_o[j].astype(BF16)

            z = _rmsnorm(hp, g_attn_norm[j], BF16)
            cq, ck, kp, kk = _kv_proj(z, wdq, wkv, g_q[j], g_kv[j], cos_p, sin_p)
            q = _q_proj(cq, wn, wp, wps, wuk_t, cos_p, sin_p, tm=TM_P)
            o = _prompt_attn(q, kk, wuv_t, B, S)
            hp, = _matmul([o], [(0, wo, 0)], [(hp, "tile")],
                          lambda accs, ex: [ex[0] + accs[0]], [F32],
                          n_cols=D_MODEL, tm=TM_P, tn=512, name="attn_out")
            ckv_p.append(ck.reshape(B, S, KV_LORA))
            kpe_p.append(kp.reshape(B, S, ROPE_DIM))

            z = _rmsnorm(hs, g_attn_norm[j], BF16)
            cq, ck, kp, kk = _kv_proj(z, wdq, wkv, g_q[j], g_kv[j], cos_s, sin_s)
            q = _q_proj(cq, wn, wp, wps, wuk_t, cos_s, sin_s, tm=TM_S)
            knew = jnp.pad(kk.reshape(DB, T, QK_PAD), ((0, 0), (0, DEC_NEW_PAD - T), (0, 0)))
            o_lat = _decode_attn(q.reshape(DB, T * N_HEADS, QK_PAD), knew,
                                 cache_ckv, cache_kpe, page_table, j, T)
            o = _uv_proj(o_lat.reshape(MS, N_HEADS * KV_LORA), wuv_t, tm=TM_S)
            hs, = _matmul([o], [(0, wo, 0)], [(hs, "tile")],
                          lambda accs, ex: [ex[0] + accs[0]], [F32],
                          n_cols=D_MODEL, tm=TM_S, tn=512, name="attn_out")
            ckv_s.append(ck.reshape(DB, T, KV_LORA))
            kpe_s.append(kp.reshape(DB, T, ROPE_DIM))
        else:
            w1 = w_pw1[j].astype(BF16)
            w2 = w_pw2[j].astype(BF16)
            b1 = b_pw1[j].reshape(1, -1)
            b2 = b_pw2[j].reshape(1, -1)
            glu = lambda accs, ex: [(accs[0] + ex[0]) * jax.nn.sigmoid(accs[1] + ex[1])]
            pw2 = lambda accs, ex: [ex[0] + accs[0] + ex[1]]
            n_half = D_MODEL // 512

            def pw1_glu(h, tm):
                z = _rmsnorm(h, g_conv_norm[j], BF16)
                u, = _matmul([z], [(0, w1, 0), (0, w1, n_half)],
                             [(b1[:, :D_MODEL], "col"), (b1[:, D_MODEL:], "col")],
                             glu, [F32], n_cols=D_MODEL, tm=tm, tn=512, name="pw1_glu")
                return u

            u = pw1_glu(hp, TM_P)
            c = _conv_prompt(u, w_dw[j], b_dw[j], g_conv_ln[j], b_conv_ln[j], B, S)
            hp, = _matmul([c], [(0, w2, 0)], [(hp, "tile"), (b2, "col")], pw2, [F32],
                          n_cols=D_MODEL, tm=TM_P, tn=512, name="pw2")
            conv_p.append(u.reshape(B, S, D_MODEL)[:, S - (CONV_W - 1):])

            u = pw1_glu(hs, TM_S)
            c, ns = _conv_sample(state_conv[j], u.reshape(DB, T, D_MODEL),
                                 w_dw[j], b_dw[j], g_conv_ln[j], b_conv_ln[j])
            hs, = _matmul([c.reshape(MS, D_MODEL)], [(0, w2, 0)], [(hs, "tile"), (b2, "col")],
                          pw2, [F32], n_cols=D_MODEL, tm=TM_S, tn=512, name="pw2")
            conv_s.append(ns)

        wg = w_ffn_gate[i].astype(BF16)
        wu = w_ffn_up[i].astype(BF16)
        wd = w_ffn_down[i].astype(BF16)
        hp = _ffn(hp, g_ffn_norm[i], wg, wu, wd, TM_P)
        hs = _ffn(hs, g_ffn_norm[i], wg, wu, wd, TM_S)
        wpg = w_ple_gate[i].astype(BF16)
        wpu = w_ple_up[i].astype(BF16)
        hp = _ple(hp, p_prompt[i].reshape(MP, -1), g_ple_norm[i], wpg, wpu, TM_P)
        hs = _ple(hs, p_sample[i].reshape(MS, -1), g_ple_norm[i], wpg, wpu, TM_S)

    y_prompt = _rmsnorm(hp, g_final, F32).reshape(B, S, D_MODEL)
    y_sample = _rmsnorm(hs, g_final, F32).reshape(DB, T, D_MODEL)
    return (y_prompt, y_sample,
            jnp.stack(ckv_p), jnp.stack(kpe_p), jnp.stack(conv_p),
            jnp.stack(ckv_s), jnp.stack(kpe_s), jnp.stack(conv_s))
```

```python
import functools
import math

import jax
import jax.numpy as jnp
from jax import lax
from jax.experimental import pallas as pl
from jax.experimental.pallas import tpu as pltpu

D_MODEL = 4096
N_HEADS = 32
Q_LORA = 1024
KV_LORA = 512
NOPE_DIM = 128
ROPE_DIM = 64
V_DIM = 128
QK_DIM = NOPE_DIM + ROPE_DIM
SM_SCALE = 1.0 / math.sqrt(QK_DIM)
ROPE_THETA = 10000.0
CONV_W = 31
EPS = 1e-6

LANES = 128
ROPE_PAD = LANES
QK_PAD = KV_LORA + ROPE_PAD
NEG_BIG = -1e30
VMEM_LIMIT = 56 * 1024 * 1024

F32 = jnp.float32
BF16 = jnp.bfloat16


def _cparams(n_axes):
    return pltpu.CompilerParams(dimension_semantics=("arbitrary",) * n_axes,
                                vmem_limit_bytes=VMEM_LIMIT)


def _dot(a, b):
    return jnp.dot(a, b, preferred_element_type=F32)


def _dot_nt(a, b):
    return lax.dot_general(a, b, (((1,), (1,)), ((), ())), preferred_element_type=F32)


def _rmsnorm_kernel(x_ref, g_ref, o_ref):
    x = x_ref[...]
    y = x * lax.rsqrt(jnp.mean(x * x, axis=-1, keepdims=True) + EPS)
    o_ref[...] = (y * g_ref[...]).astype(o_ref.dtype)


def _rmsnorm(x, g, out_dtype, tm=256):
    m, d = x.shape
    tm = min(tm, m)
    return pl.pallas_call(
        _rmsnorm_kernel,
        out_shape=jax.ShapeDtypeStruct((m, d), out_dtype),
        grid=(m // tm,),
        in_specs=[pl.BlockSpec((tm, d), lambda i: (i, 0)),
                  pl.BlockSpec((1, d), lambda i: (0, 0))],
        out_specs=pl.BlockSpec((tm, d), lambda i: (i, 0)),
        compiler_params=_cparams(1),
        name="rmsnorm",
    )(x, g.reshape(1, d))


def _matmul(xs, wspecs, extras, epilogue, out_dtypes, *, n_cols, tm, tn, name, emit_w=False):
    m = xs[0].shape[0]
    tm = min(tm, m)
    nx, nw, ne, no = len(xs), len(wspecs), len(extras), len(out_dtypes)
    assert not emit_w or m == tm
    in_specs = [pl.BlockSpec((tm, x.shape[1]), lambda i, j: (i, 0)) for x in xs]
    for _, w, layer, off in wspecs:
        in_specs.append(pl.BlockSpec((None, w.shape[1], tn),
                                     lambda i, j, layer=layer, off=off: (layer, 0, j + off)))
    for _, kind in extras:
        if kind == "tile":
            in_specs.append(pl.BlockSpec((tm, tn), lambda i, j: (i, j)))
        else:
            in_specs.append(pl.BlockSpec((1, tn), lambda i, j: (0, j)))
    x_index = [xi for xi, _, _, _ in wspecs]

    def kernel(*refs):
        x_refs, w_refs = refs[:nx], refs[nx:nx + nw]
        e_refs = refs[nx + nw:nx + nw + ne]
        o_refs, wo_refs = refs[nx + nw + ne:nx + nw + ne + no], refs[nx + nw + ne + no:]
        xv = [r[...].astype(BF16) for r in x_refs]
        wv = [r[...].astype(BF16) for r in w_refs]
        accs = [_dot(xv[xi], w) for xi, w in zip(x_index, wv)]
        outs = epilogue(accs, [e[...] for e in e_refs])
        for o_ref, o in zip(o_refs, outs):
            o_ref[...] = o.astype(o_ref.dtype)
        for wo_ref, w in zip(wo_refs, wv):
            wo_ref[...] = w

    out_shape = [jax.ShapeDtypeStruct((m, n_cols), dt) for dt in out_dtypes]
    out_specs = [pl.BlockSpec((tm, tn), lambda i, j: (i, j)) for _ in range(no)]
    if emit_w:
        for _, w, _, _ in wspecs:
            out_shape.append(jax.ShapeDtypeStruct((1, w.shape[1], n_cols), BF16))
            out_specs.append(pl.BlockSpec((None, w.shape[1], tn), lambda i, j: (0, 0, j)))
    outs = pl.pallas_call(
        kernel,
        out_shape=out_shape,
        grid=(m // tm, n_cols // tn),
        in_specs=in_specs,
        out_specs=out_specs,
        compiler_params=_cparams(2),
        name=name,
    )(*xs, *[w for _, w, _, _ in wspecs], *[a for a, _ in extras])
    return (outs[:no], outs[no:]) if emit_w else outs


def _kv_proj_kernel(z_ref, wdq_ref, wkv_ref, gq_ref, gkv_ref, cos_ref, sin_ref,
                    cq_ref, ckv_ref, kpe_ref, k_ref):
    z = z_ref[...]
    a = _dot(z, wdq_ref[...])
    cq = a * lax.rsqrt(jnp.mean(a * a, axis=-1, keepdims=True) + EPS) * gq_ref[...]
    cq_ref[...] = cq.astype(cq_ref.dtype)
    kv = _dot(z, wkv_ref[...])
    c = kv[:, :KV_LORA]
    ckv = c * lax.rsqrt(jnp.mean(c * c, axis=-1, keepdims=True) + EPS) * gkv_ref[...]
    kpe = (kv[:, KV_LORA:KV_LORA + ROPE_PAD] * cos_ref[...]
           + kv[:, KV_LORA + ROPE_PAD:] * sin_ref[...])
    ckv_ref[...] = ckv
    kpe_ref[...] = kpe[:, :ROPE_DIM]
    k_ref[:, :KV_LORA] = ckv.astype(k_ref.dtype)
    k_ref[:, KV_LORA:] = kpe.astype(k_ref.dtype)


def _kv_proj(z, wdq, wkv, gq, gkv, cos, sin, tm=512):
    m = z.shape[0]
    tm = min(tm, m)
    row = lambda i: (i, 0)
    fixed = lambda i: (0, 0)
    return pl.pallas_call(
        _kv_proj_kernel,
        out_shape=[jax.ShapeDtypeStruct((m, Q_LORA), BF16),
                   jax.ShapeDtypeStruct((m, KV_LORA), F32),
                   jax.ShapeDtypeStruct((m, ROPE_DIM), F32),
                   jax.ShapeDtypeStruct((m, QK_PAD), BF16)],
        grid=(m // tm,),
        in_specs=[pl.BlockSpec((tm, D_MODEL), row),
                  pl.BlockSpec(wdq.shape, fixed),
                  pl.BlockSpec(wkv.shape, fixed),
                  pl.BlockSpec((1, Q_LORA), fixed),
                  pl.BlockSpec((1, KV_LORA), fixed),
                  pl.BlockSpec((tm, ROPE_PAD), row),
                  pl.BlockSpec((tm, ROPE_PAD), row)],
        out_specs=[pl.BlockSpec((tm, Q_LORA), row),
                   pl.BlockSpec((tm, KV_LORA), row),
                   pl.BlockSpec((tm, ROPE_DIM), row),
                   pl.BlockSpec((tm, QK_PAD), row)],
        compiler_params=_cparams(1),
        name="kv_proj",
    )(z, wdq, wkv, gq.reshape(1, -1), gkv.reshape(1, -1), cos, sin)


Q_HEADS_PER_STEP = 4


def _q_proj_kernel(cq_ref, wn_ref, wp_ref, wps_ref, wuk_ref, cos_ref, sin_ref, q_ref):
    cq = cq_ref[...]
    qn = _dot(cq, wn_ref[...]).astype(BF16)
    qp = _dot(cq, wp_ref[...])
    qps = _dot(cq, wps_ref[...])
    cos, sin = cos_ref[...], sin_ref[...]
    for h in range(Q_HEADS_PER_STEP):
        ql = _dot(qn[:, h * NOPE_DIM:(h + 1) * NOPE_DIM], wuk_ref[h])
        qr = (qp[:, h * ROPE_PAD:(h + 1) * ROPE_PAD] * cos
              + qps[:, h * ROPE_PAD:(h + 1) * ROPE_PAD] * sin)
        q_ref[:, h * QK_PAD:h * QK_PAD + KV_LORA] = ql.astype(q_ref.dtype)
        q_ref[:, h * QK_PAD + KV_LORA:(h + 1) * QK_PAD] = qr.astype(q_ref.dtype)


def _q_proj(cq, wn, wp, wps, wuk_t, cos, sin, tm):
    m = cq.shape[0]
    tm = min(tm, m)
    hq = Q_HEADS_PER_STEP
    return pl.pallas_call(
        _q_proj_kernel,
        out_shape=jax.ShapeDtypeStruct((m, N_HEADS * QK_PAD), BF16),
        grid=(m // tm, N_HEADS // hq),
        in_specs=[pl.BlockSpec((tm, Q_LORA), lambda i, j: (i, 0)),
                  pl.BlockSpec((Q_LORA, hq * NOPE_DIM), lambda i, j: (0, j)),
                  pl.BlockSpec((Q_LORA, hq * ROPE_PAD), lambda i, j: (0, j)),
                  pl.BlockSpec((Q_LORA, hq * ROPE_PAD), lambda i, j: (0, j)),
                  pl.BlockSpec((hq, NOPE_DIM, KV_LORA), lambda i, j: (j, 0, 0)),
                  pl.BlockSpec((tm, ROPE_PAD), lambda i, j: (i, 0)),
                  pl.BlockSpec((tm, ROPE_PAD), lambda i, j: (i, 0))],
        out_specs=pl.BlockSpec((tm, hq * QK_PAD), lambda i, j: (i, j)),
        compiler_params=_cparams(2),
        name="q_proj",
    )(cq, wn, wp, wps, wuk_t, cos, sin)


def _softmax_update(s, v, m_sc, l_sc, acc_sc):
    n = s.shape[1]
    m_prev = m_sc[...]
    m_new = jnp.maximum(m_prev, jnp.max(s, axis=1, keepdims=True))
    alpha = jnp.exp(m_prev - m_new)
    m_wide = jnp.tile(m_new, (1, n // LANES)) if n % LANES == 0 else m_new[:, :n]
    p = jnp.exp(s - m_wide)
    l_sc[...] = alpha * l_sc[...] + jnp.sum(p, axis=1, keepdims=True)
    acc_sc[...] = jnp.tile(alpha, (1, KV_LORA // LANES)) * acc_sc[...] + _dot(p.astype(BF16), v)
    m_sc[...] = m_new


PROMPT_TQ = 512
PROMPT_HG = 4


def _prompt_attn_kernel(q_ref, k_ref, wuv_ref, o_ref, m_sc, l_sc, acc_sc):
    tq, hg = PROMPT_TQ, PROMPT_HG
    i = pl.program_id(1)
    q = jnp.concatenate([q_ref[:, h * QK_PAD:(h + 1) * QK_PAD] for h in range(hg)], axis=0)
    m_sc[...] = jnp.full(m_sc.shape, -jnp.inf, F32)
    l_sc[...] = jnp.zeros(l_sc.shape, F32)
    acc_sc[...] = jnp.zeros(acc_sc.shape, F32)

    def full_tile(kt, carry):
        k = k_ref[pl.ds(pl.multiple_of(kt * tq, tq), tq), :]
        s = _dot_nt(q, k) * SM_SCALE
        _softmax_update(s, k[:, :KV_LORA], m_sc, l_sc, acc_sc)
        return carry

    lax.fori_loop(0, i, full_tile, 0)

    k = k_ref[pl.ds(pl.multiple_of(i * tq, tq), tq), :]
    s = _dot_nt(q, k) * SM_SCALE
    tok = lax.broadcasted_iota(jnp.int32, s.shape, 0) & (tq - 1)
    key = lax.broadcasted_iota(jnp.int32, s.shape, 1)
    s = jnp.where(key <= tok, s, NEG_BIG)
    _softmax_update(s, k[:, :KV_LORA], m_sc, l_sc, acc_sc)

    o_lat = (acc_sc[...] / jnp.tile(l_sc[...], (1, KV_LORA // LANES))).astype(BF16)
    for h in range(hg):
        o_ref[:, h * V_DIM:(h + 1) * V_DIM] = _dot(
            o_lat[h * tq:(h + 1) * tq, :], wuv_ref[h]).astype(o_ref.dtype)


def _prompt_attn(q, k, wuv_t, batch, seq):
    tq, hg = PROMPT_TQ, PROMPT_HG
    nq = seq // tq
    rows = tq * hg
    return pl.pallas_call(
        _prompt_attn_kernel,
        out_shape=jax.ShapeDtypeStruct((batch * seq, N_HEADS * V_DIM), BF16),
        grid=(batch, nq, N_HEADS // hg),
        in_specs=[pl.BlockSpec((tq, hg * QK_PAD), lambda b, i, g: (b * nq + i, g)),
                  pl.BlockSpec((seq, QK_PAD), lambda b, i, g: (b, 0)),
                  pl.BlockSpec((hg, KV_LORA, V_DIM), lambda b, i, g: (g, 0, 0))],
        out_specs=pl.BlockSpec((tq, hg * V_DIM), lambda b, i, g: (b * nq + i, g)),
        scratch_shapes=[pltpu.VMEM((rows, LANES), F32), pltpu.VMEM((rows, LANES), F32),
                        pltpu.VMEM((rows, KV_LORA), F32)],
        compiler_params=_cparams(3),
        name="prompt_attn",
    )(q, k, wuv_t)


DEC_PAGES = 16
DEC_NEW_PAD = 16


def _decode_attn_kernel(pt_ref, q_ref, knew_ref, ck_hbm, kp_hbm, o_ref,
                        ck_buf, kp_buf, sem, m_sc, l_sc, acc_sc, *, layer, n_new, n_chunks, page):
    b = pl.program_id(0)
    n_b = pl.num_programs(0)

    def chunk_copies(bb, c, slot, lookup):
        copies = []
        for p in range(DEC_PAGES):
            pg = pt_ref[bb, c * DEC_PAGES + p] if lookup else 0
            rows = pl.ds(p * page, page)
            copies.append(pltpu.make_async_copy(
                ck_hbm.at[layer, pg], ck_buf.at[slot, rows, :], sem.at[slot]))
            copies.append(pltpu.make_async_copy(
                kp_hbm.at[layer, pg], kp_buf.at[slot, :, rows], sem.at[slot]))
        return copies

    @pl.when(b == 0)
    def _():
        for cp in chunk_copies(0, 0, 0, True):
            cp.start()

    m_sc[...] = jnp.full(m_sc.shape, -jnp.inf, F32)
    l_sc[...] = jnp.zeros(l_sc.shape, F32)
    acc_sc[...] = jnp.zeros(acc_sc.shape, F32)
    q = q_ref[0]
    q_lat, q_pe = q[:, :KV_LORA], q[:, KV_LORA:KV_LORA + ROPE_DIM]

    def chunk(c, carry):
        slot = c & 1
        last = c + 1 == n_chunks
        nxt_b = jnp.where(last, b + 1, b)
        nxt_c = jnp.where(last, 0, c + 1)

        @pl.when(nxt_b < n_b)
        def _():
            for cp in chunk_copies(nxt_b, nxt_c, 1 - slot, True):
                cp.start()

        for cp in chunk_copies(b, c, slot, False):
            cp.wait()
        ck = ck_buf[slot].astype(BF16)
        kp_t = kp_buf[slot].astype(BF16)
        s = (_dot_nt(q_lat, ck) + _dot(q_pe, kp_t)) * SM_SCALE
        _softmax_update(s, ck, m_sc, l_sc, acc_sc)
        return carry

    lax.fori_loop(0, n_chunks, chunk, 0)

    kn = knew_ref[0]
    sn = _dot_nt(q, kn) * SM_SCALE
    tok = lax.broadcasted_iota(jnp.int32, sn.shape, 0) >> (N_HEADS.bit_length() - 1)
    key = lax.broadcasted_iota(jnp.int32, sn.shape, 1)
    sn = jnp.where((key <= tok) & (key < n_new), sn, NEG_BIG)
    _softmax_update(sn, kn[:, :KV_LORA], m_sc, l_sc, acc_sc)
    o_ref[0] = (acc_sc[...] / jnp.tile(l_sc[...], (1, KV_LORA // LANES))).astype(o_ref.dtype)


def _decode_attn(q, knew, cache_ckv, cache_kpe_t, page_table, layer, n_new):
    db, rows, _ = q.shape
    n_pages = page_table.shape[1]
    page = cache_ckv.shape[2]
    n_chunks = n_pages // DEC_PAGES
    assert n_pages % DEC_PAGES == 0 and n_chunks % 2 == 0
    keys = DEC_PAGES * page
    return pl.pallas_call(
        functools.partial(_decode_attn_kernel, layer=layer, n_new=n_new, n_chunks=n_chunks, page=page),
        out_shape=jax.ShapeDtypeStruct((db, rows, KV_LORA), BF16),
        grid_spec=pltpu.PrefetchScalarGridSpec(
            num_scalar_prefetch=1,
            grid=(db,),
            in_specs=[pl.BlockSpec((1, rows, QK_PAD), lambda b, pt: (b, 0, 0)),
                      pl.BlockSpec((1, DEC_NEW_PAD, QK_PAD), lambda b, pt: (b, 0, 0)),
                      pl.BlockSpec(memory_space=pl.ANY),
                      pl.BlockSpec(memory_space=pl.ANY)],
            out_specs=pl.BlockSpec((1, rows, KV_LORA), lambda b, pt: (b, 0, 0)),
            scratch_shapes=[pltpu.VMEM((2, keys, KV_LORA), F32),
                            pltpu.VMEM((2, ROPE_DIM, keys), F32),
                            pltpu.SemaphoreType.DMA((2,)),
                            pltpu.VMEM((rows, LANES), F32), pltpu.VMEM((rows, LANES), F32),
                            pltpu.VMEM((rows, KV_LORA), F32)]),
        compiler_params=_cparams(1),
        name="decode_attn",
    )(page_table, q, knew, cache_ckv, cache_kpe_t)


UV_HEADS_PER_STEP = 4


def _uv_proj_kernel(x_ref, w_ref, o_ref):
    for h in range(UV_HEADS_PER_STEP):
        o_ref[:, h * V_DIM:(h + 1) * V_DIM] = _dot(
            x_ref[:, h * KV_LORA:(h + 1) * KV_LORA], w_ref[h]).astype(o_ref.dtype)


def _uv_proj(o_lat, wuv_t, tm):
    m = o_lat.shape[0]
    hu = UV_HEADS_PER_STEP
    return pl.pallas_call(
        _uv_proj_kernel,
        out_shape=jax.ShapeDtypeStruct((m, N_HEADS * V_DIM), BF16),
        grid=(m // tm, N_HEADS // hu),
        in_specs=[pl.BlockSpec((tm, hu * KV_LORA), lambda i, j: (i, j)),
                  pl.BlockSpec((hu, KV_LORA, V_DIM), lambda i, j: (j, 0, 0))],
        out_specs=pl.BlockSpec((tm, hu * V_DIM), lambda i, j: (i, j)),
        compiler_params=_cparams(2),
        name="uv_proj",
    )(o_lat, wuv_t)


CONV_TS = 256
CONV_HALO = 32
CONV_RC = 128
CONV_LN_ROWS = 64
SUBLANES = 8


def _ln_silu(y, g, b):
    yc = y - jnp.mean(y, axis=-1, keepdims=True)
    yn = yc * lax.rsqrt(jnp.mean(yc * yc, axis=-1, keepdims=True) + EPS) * g + b
    return yn * jax.nn.sigmoid(yn)


def _conv_prompt_kernel(halo_ref, x_ref, w_ref, bdw_ref, g_ref, b_ref, o_ref, xw_sc, y_sc, sh_sc):
    ts = CONV_TS
    i = pl.program_id(1)
    xw_sc[0:CONV_HALO, :] = jnp.where(i > 0, halo_ref[...], 0.0)
    xw_sc[CONV_HALO:CONV_HALO + ts, :] = x_ref[...]
    base = CONV_HALO - (CONV_W - 1)

    def col_chunk(c, carry):
        col = pl.multiple_of(c * LANES, LANES)
        for rc in range(ts // CONV_RC):
            row0 = rc * CONV_RC
            acc = jnp.zeros((CONV_RC, LANES), F32)
            for r in range(SUBLANES):
                taps = [(o // SUBLANES, o - base) for o in range(r, CONV_HALO + 1, SUBLANES)
                        if 0 <= o - base < CONV_W]
                j_lo, j_hi = taps[0][0], taps[-1][0]
                start = row0 + r + j_lo * SUBLANES
                n_rows = (j_hi - j_lo) * SUBLANES + CONV_RC
                sh_sc[r, 0:n_rows, :] = xw_sc[start:start + n_rows, pl.ds(col, LANES)]
                for j, k in taps:
                    off = (j - j_lo) * SUBLANES
                    acc = acc + w_ref[k:k + 1, pl.ds(col, LANES)] * sh_sc[r, off:off + CONV_RC, :]
            y_sc[row0:row0 + CONV_RC, pl.ds(col, LANES)] = acc + bdw_ref[:, pl.ds(col, LANES)]
        return carry

    lax.fori_loop(0, D_MODEL // LANES, col_chunk, 0)

    def ln_chunk(r, carry):
        rs = pl.ds(pl.multiple_of(r * CONV_LN_ROWS, CONV_LN_ROWS), CONV_LN_ROWS)
        o_ref[rs, :] = _ln_silu(y_sc[rs, :], g_ref[...], b_ref[...]).astype(o_ref.dtype)
        return carry

    lax.fori_loop(0, ts // CONV_LN_ROWS, ln_chunk, 0)


def _conv_prompt(u, w_dw, b_dw, g_ln, b_ln, batch, seq):
    ts = CONV_TS
    nt = seq // ts
    hb = ts // CONV_HALO
    fixed = lambda b, i: (0, 0)
    return pl.pallas_call(
        _conv_prompt_kernel,
        out_shape=jax.ShapeDtypeStruct((batch * seq, D_MODEL), BF16),
        grid=(batch, nt),
        in_specs=[pl.BlockSpec((CONV_HALO, D_MODEL),
                               lambda b, i: (jnp.maximum((b * nt + i) * hb - 1, 0), 0)),
                  pl.BlockSpec((ts, D_MODEL), lambda b, i: (b * nt + i, 0)),
                  pl.BlockSpec((CONV_W, D_MODEL), fixed),
                  pl.BlockSpec((1, D_MODEL), fixed),
                  pl.BlockSpec((1, D_MODEL), fixed),
                  pl.BlockSpec((1, D_MODEL), fixed)],
        out_specs=pl.BlockSpec((ts, D_MODEL), lambda b, i: (b * nt + i, 0)),
        scratch_shapes=[pltpu.VMEM((CONV_HALO + ts, D_MODEL), F32),
                        pltpu.VMEM((ts, D_MODEL), F32),
                        pltpu.VMEM((SUBLANES, CONV_HALO + CONV_RC, LANES), F32)],
        compiler_params=_cparams(2),
        name="conv_prompt",
    )(u, u, w_dw, b_dw.reshape(1, -1), g_ln.reshape(1, -1), b_ln.reshape(1, -1))


CONV_SB = 8


def _conv_sample_kernel(st_ref, u_ref, w_ref, bdw_ref, g_ref, b_ref, o_ref, ns_ref, full_sc):
    n_prev = CONV_W - 1
    t = u_ref.shape[1]

    def one(bi, carry):
        full_sc[0:n_prev, :] = st_ref[bi]
        full_sc[n_prev:n_prev + t, :] = u_ref[bi]
        acc = jnp.zeros((t, D_MODEL), F32)
        for k in range(CONV_W):
            acc = acc + w_ref[k:k + 1, :] * full_sc[k:k + t, :]
        o_ref[bi] = _ln_silu(acc + bdw_ref[...], g_ref[...], b_ref[...]).astype(o_ref.dtype)
        ns_ref[bi] = full_sc[t:t + n_prev, :]
        return carry

    lax.fori_loop(0, CONV_SB, one, 0)


def _conv_sample(state, u, w_dw, b_dw, g_ln, b_ln):
    db, n_prev, d = state.shape
    t = u.shape[1]
    sb = CONV_SB
    fixed = lambda i: (0, 0)
    blk = lambda i: (i, 0, 0)
    return pl.pallas_call(
        _conv_sample_kernel,
        out_shape=[jax.ShapeDtypeStruct((db, t, d), F32),
                   jax.ShapeDtypeStruct((db, n_prev, d), F32)],
        grid=(db // sb,),
        in_specs=[pl.BlockSpec((sb, n_prev, d), blk),
                  pl.BlockSpec((sb, t, d), blk),
                  pl.BlockSpec((CONV_W, d), fixed),
                  pl.BlockSpec((1, d), fixed),
                  pl.BlockSpec((1, d), fixed),
                  pl.BlockSpec((1, d), fixed)],
        out_specs=[pl.BlockSpec((sb, t, d), blk), pl.BlockSpec((sb, n_prev, d), blk)],
        scratch_shapes=[pltpu.VMEM((n_prev + t + 6, d), F32)],
        compiler_params=_cparams(1),
        name="conv_sample",
    )(state, u, w_dw, b_dw.reshape(1, -1), g_ln.reshape(1, -1), b_ln.reshape(1, -1))


TM_PROMPT = 1024


def _residual_add(accs, ex):
    out = ex[0] + accs[0]
    for e in ex[1:]:
        out = out + e
    return [out]


def _linear_residual(xs, xp, hs, hp, w, layer, name, bias=None):
    ms = xs.shape[0]
    cols = [] if bias is None else [(bias.reshape(1, -1), "col")]
    (hs2,), (wb,) = _matmul([xs], [(0, w, layer, 0)], [(hs, "tile")] + cols, _residual_add, [F32],
                            n_cols=D_MODEL, tm=ms, tn=256, name=name, emit_w=True)
    hp2, = _matmul([xp], [(0, wb, 0, 0)], [(hp, "tile")] + cols, _residual_add, [F32],
                   n_cols=D_MODEL, tm=TM_PROMPT, tn=512, name=name)
    return hs2, hp2


def _ffn(hs, hp, g, wg, wu, wd, layer):
    ms = hs.shape[0]
    d_ff = wg.shape[2]
    gate_up = lambda accs, ex: [jax.nn.silu(accs[0]) * accs[1]]
    zs, zp = _rmsnorm(hs, g, BF16), _rmsnorm(hp, g, BF16)
    (act_s,), (wgb, wub) = _matmul([zs], [(0, wg, layer, 0), (0, wu, layer, 0)], [], gate_up, [BF16],
                                   n_cols=d_ff, tm=ms, tn=256, name="ffn_gate_up", emit_w=True)
    act_p, = _matmul([zp], [(0, wgb, 0, 0), (0, wub, 0, 0)], [], gate_up, [BF16],
                     n_cols=d_ff, tm=TM_PROMPT, tn=256, name="ffn_gate_up")
    (hs2,), (wdb,) = _matmul([act_s], [(0, wd, layer, 0)], [(hs, "tile")], _residual_add, [F32],
                             n_cols=D_MODEL, tm=ms, tn=128, name="ffn_down", emit_w=True)
    hp2, = _matmul([act_p], [(0, wdb, 0, 0)], [(hp, "tile")], _residual_add, [F32],
                   n_cols=D_MODEL, tm=512, tn=256, name="ffn_down")
    return hs2, hp2


def _ple(hs, hp, ps, pp, g, w_gate, w_up, layer):
    ms = hs.shape[0]
    gated = lambda accs, ex: [ex[0] + jax.nn.sigmoid(accs[0]) * accs[1]]
    zs, zp = _rmsnorm(hs, g, BF16), _rmsnorm(hp, g, BF16)
    (hs2,), (wgb, wub) = _matmul([zs, ps], [(0, w_gate, layer, 0), (1, w_up, layer, 0)],
                                 [(hs, "tile")], gated, [F32],
                                 n_cols=D_MODEL, tm=ms, tn=256, name="ple", emit_w=True)
    hp2, = _matmul([zp, pp], [(0, wgb, 0, 0), (1, wub, 0, 0)], [(hp, "tile")], gated, [F32],
                   n_cols=D_MODEL, tm=TM_PROMPT, tn=512, name="ple")
    return hs2, hp2


def _rope_tables(pos):
    half = ROPE_DIM // 2
    inv_freq = ROPE_THETA ** (-jnp.arange(half, dtype=F32) / half)
    ang = pos.astype(F32)[:, None] * inv_freq
    cos, sin = jnp.cos(ang), jnp.sin(ang)
    pad = jnp.zeros((pos.shape[0], ROPE_PAD - ROPE_DIM), F32)
    return (jnp.concatenate([cos, cos, pad], axis=1),
            jnp.concatenate([-sin, sin, pad], axis=1))


def _swap_halves(w):
    half = ROPE_DIM // 2
    return jnp.concatenate([w[..., half:], w[..., :half]], axis=-1)


def _pad_rope(w):
    return jnp.concatenate([w, jnp.zeros(w.shape[:-1] + (ROPE_PAD - ROPE_DIM,), w.dtype)], axis=-1)


def kernel(x_prompt, x_sample, cache_ckv, cache_kpe, state_conv, page_table, p_prompt, p_sample,
           g_attn_norm, w_dq, g_q, w_uq, w_dkv, g_kv, w_uk, w_uv, w_o,
           g_conv_norm, w_pw1, b_pw1, w_dw, b_dw, g_conv_ln, b_conv_ln, w_pw2, b_pw2,
           g_ffn_norm, w_ffn_gate, w_ffn_up, w_ffn_down,
           g_ple_norm, w_ple_gate, w_ple_up, g_final):
    B, S, _ = x_prompt.shape
    DB, T, _ = x_sample.shape
    depth = g_ffn_norm.shape[0]
    past = page_table.shape[1] * cache_ckv.shape[2]
    MP, MS = B * S, DB * T

    cos_p, sin_p = _rope_tables(jnp.tile(jnp.arange(S, dtype=jnp.int32), B))
    cos_s, sin_s = _rope_tables(jnp.tile(past + jnp.arange(T, dtype=jnp.int32), DB))

    hp = x_prompt.reshape(MP, D_MODEL)
    hs = x_sample.reshape(MS, D_MODEL)
    ckv_p, kpe_p, conv_p, ckv_s, kpe_s, conv_s = [], [], [], [], [], []

    for i in range(depth):
        j = i // 2
        if i % 2 == 0:
            wdq = w_dq[j].astype(BF16)
            w_kpe = w_dkv[j][:, KV_LORA:]
            wkv = jnp.concatenate([w_dkv[j][:, :KV_LORA], _pad_rope(w_kpe),
                                   _pad_rope(_swap_halves(w_kpe))], axis=1).astype(BF16)
            wq3 = w_uq[j].reshape(Q_LORA, N_HEADS, QK_DIM)
            wn = wq3[:, :, :NOPE_DIM].reshape(Q_LORA, N_HEADS * NOPE_DIM).astype(BF16)
            wp = _pad_rope(wq3[:, :, NOPE_DIM:]).reshape(Q_LORA, N_HEADS * ROPE_PAD).astype(BF16)
            wps = _pad_rope(_swap_halves(wq3[:, :, NOPE_DIM:])).reshape(
                Q_LORA, N_HEADS * ROPE_PAD).astype(BF16)
            wuk_t = jnp.transpose(w_uk[j], (1, 2, 0)).astype(BF16)
            wuv_t = jnp.transpose(w_uv[j], (1, 0, 2)).astype(BF16)

            z = _rmsnorm(hs, g_attn_norm[j], BF16)
            cq, ck, kp, kk = _kv_proj(z, wdq, wkv, g_q[j], g_kv[j], cos_s, sin_s)
            q = _q_proj(cq, wn, wp, wps, wuk_t, cos_s, sin_s, tm=MS)
            knew = jnp.pad(kk.reshape(DB, T, QK_PAD), ((0, 0), (0, DEC_NEW_PAD - T), (0, 0)))
            o_lat = _decode_attn(q.reshape(DB, T * N_HEADS, QK_PAD), knew,
                                 cache_ckv, jnp.swapaxes(cache_kpe, 2, 3), page_table, j, T)
            o_s = _uv_proj(o_lat.reshape(MS, N_HEADS * KV_LORA), wuv_t, tm=MS)
            ckv_s.append(ck.reshape(DB, T, KV_LORA))
            kpe_s.append(kp.reshape(DB, T, ROPE_DIM))

            z = _rmsnorm(hp, g_attn_norm[j], BF16)
            cq, ck, kp, kk = _kv_proj(z, wdq, wkv, g_q[j], g_kv[j], cos_p, sin_p)
            q = _q_proj(cq, wn, wp, wps, wuk_t, cos_p, sin_p, tm=TM_PROMPT)
            o_p = _prompt_attn(q, kk, wuv_t, B, S)
            ckv_p.append(ck.reshape(B, S, KV_LORA))
            kpe_p.append(kp.reshape(B, S, ROPE_DIM))

            hs, hp = _linear_residual(o_s, o_p, hs, hp, w_o, j, "attn_out")
        else:
            b1 = b_pw1[j].reshape(1, -1)
            glu = lambda accs, ex: [(accs[0] + ex[0]) * jax.nn.sigmoid(accs[1] + ex[1])]
            biases = [(b1[:, :D_MODEL], "col"), (b1[:, D_MODEL:], "col")]
            zs = _rmsnorm(hs, g_conv_norm[j], BF16)
            zp = _rmsnorm(hp, g_conv_norm[j], BF16)
            (u_s,), (w1a, w1b) = _matmul([zs], [(0, w_pw1, j, 0), (0, w_pw1, j, D_MODEL // 256)],
                                         biases, glu, [F32], n_cols=D_MODEL, tm=MS, tn=256,
                                         name="pw1_glu", emit_w=True)
            u_p, = _matmul([zp], [(0, w1a, 0, 0), (0, w1b, 0, 0)], biases, glu, [F32],
                           n_cols=D_MODEL, tm=TM_PROMPT, tn=512, name="pw1_glu")
            c_s, ns = _conv_sample(state_conv[j], u_s.reshape(DB, T, D_MODEL),
                                   w_dw[j], b_dw[j], g_conv_ln[j], b_conv_ln[j])
            c_p = _conv_prompt(u_p, w_dw[j], b_dw[j], g_conv_ln[j], b_conv_ln[j], B, S)
            conv_s.append(ns)
            conv_p.append(u_p.reshape(B, S, D_MODEL)[:, S - (CONV_W - 1):])
            hs, hp = _linear_residual(c_s.reshape(MS, D_MODEL), c_p, hs, hp, w_pw2, j, "pw2",
                                      bias=b_pw2[j])

        hs, hp = _ffn(hs, hp, g_ffn_norm[i], w_ffn_gate, w_ffn_up, w_ffn_down, i)
        hs, hp = _ple(hs, hp, p_sample[i].reshape(MS, -1), p_prompt[i].reshape(MP, -1),
                      g_ple_norm[i], w_ple_gate, w_ple_up, i)

    y_prompt = _rmsnorm(hp, g_final, F32).reshape(B, S, D_MODEL)
    y_sample = _rmsnorm(hs, g_final, F32).reshape(DB, T, D_MODEL)
    return (y_prompt, y_sample,
            jnp.stack(ckv_p), jnp.stack(kpe_p), jnp.stack(conv_p),
            jnp.stack(ckv_s), jnp.stack(kpe_s), jnp.stack(conv_s))
```

```python
import functools
import math

import jax
import jax.numpy as jnp
from jax import lax
from jax.experimental import pallas as pl
from jax.experimental.pallas import tpu as pltpu

D_MODEL = 4096
N_HEADS = 32
Q_LORA = 1024
KV_LORA = 512
NOPE_DIM = 128
ROPE_DIM = 64
V_DIM = 128
QK_DIM = NOPE_DIM + ROPE_DIM
SM_SCALE = 1.0 / math.sqrt(QK_DIM)
ROPE_THETA = 10000.0
CONV_W = 31
EPS = 1e-6

LANES = 128
ROPE_PAD = LANES
QK_PAD = KV_LORA + ROPE_PAD
NEG_BIG = -1e30
VMEM_LIMIT = 56 * 1024 * 1024

F32 = jnp.float32
BF16 = jnp.bfloat16


def _cparams(n_axes):
    return pltpu.CompilerParams(dimension_semantics=("arbitrary",) * n_axes,
                                vmem_limit_bytes=VMEM_LIMIT)


def _dot(a, b):
    return jnp.dot(a, b, preferred_element_type=F32)


def _dot_nt(a, b):
    return lax.dot_general(a, b, (((1,), (1,)), ((), ())), preferred_element_type=F32)


def _rmsnorm_kernel(x_ref, g_ref, o_ref):
    x = x_ref[...]
    y = x * lax.rsqrt(jnp.mean(x * x, axis=-1, keepdims=True) + EPS)
    o_ref[...] = (y * g_ref[...]).astype(o_ref.dtype)


def _rmsnorm(x, g, out_dtype, tm=256):
    m, d = x.shape
    tm = min(tm, m)
    return pl.pallas_call(
        _rmsnorm_kernel,
        out_shape=jax.ShapeDtypeStruct((m, d), out_dtype),
        grid=(m // tm,),
        in_specs=[pl.BlockSpec((tm, d), lambda i: (i, 0)),
                  pl.BlockSpec((1, d), lambda i: (0, 0))],
        out_specs=pl.BlockSpec((tm, d), lambda i: (i, 0)),
        compiler_params=_cparams(1),
        name="rmsnorm",
    )(x, g.reshape(1, d))


def _matmul(xs, wspecs, extras, epilogue, out_dtypes, *, n_cols, tm, tn, name,
            emit_w=False, norm_gain=None):
    m = xs[0].shape[0]
    tm = min(tm, m)
    nx, nw, ne, no = len(xs), len(wspecs), len(extras), len(out_dtypes)
    has_norm = norm_gain is not None
    assert not emit_w or m == tm
    assert not has_norm or out_dtypes[0] == F32
    in_specs = [pl.BlockSpec((tm, x.shape[1]), lambda i, j: (i, 0)) for x in xs]
    for _, w, layer, off in wspecs:
        in_specs.append(pl.BlockSpec((None, w.shape[1], tn),
                                     lambda i, j, layer=layer, off=off: (layer, 0, j + off)))
    for _, kind in extras:
        if kind == "tile":
            in_specs.append(pl.BlockSpec((tm, tn), lambda i, j: (i, j)))
        elif kind == "row":
            in_specs.append(pl.BlockSpec((tm, LANES), lambda i, j: (i, 0)))
        else:
            in_specs.append(pl.BlockSpec((1, tn), lambda i, j: (0, j)))
    inputs = [*xs, *[w for _, w, _, _ in wspecs], *[a for a, _ in extras]]
    if has_norm:
        in_specs.append(pl.BlockSpec((1, tn), lambda i, j: (0, j)))
        inputs.append(norm_gain.reshape(1, n_cols))
    x_index = [xi for xi, _, _, _ in wspecs]
    n_in = nx + nw + ne + has_norm

    def kernel(*refs):
        x_refs, w_refs = refs[:nx], refs[nx:nx + nw]
        e_refs = refs[nx + nw:nx + nw + ne]
        o_refs = refs[n_in:n_in + no]
        rest = refs[n_in + no:]
        xv = [r[...].astype(BF16) for r in x_refs]
        wv = [r[...].astype(BF16) for r in w_refs]
        accs = [_dot(xv[xi], w) for xi, w in zip(x_index, wv)]
        outs = epilogue(accs, [e[...] for e in e_refs])
        for o_ref, o in zip(o_refs, outs):
            o_ref[...] = o.astype(o_ref.dtype)
        if has_norm:
            hb_ref, ssq_ref = rest[:2]
            rest = rest[2:]
            h = outs[0]
            hb_ref[...] = (h * refs[n_in - 1][...]).astype(BF16)
            part = jnp.broadcast_to(jnp.sum(h * h, axis=1, keepdims=True), ssq_ref.shape)

            @pl.when(pl.program_id(1) == 0)
            def _():
                ssq_ref[...] = part

            @pl.when(pl.program_id(1) > 0)
            def _():
                ssq_ref[...] += part
        for wo_ref, w in zip(rest, wv):
            wo_ref[...] = w

    out_shape = [jax.ShapeDtypeStruct((m, n_cols), dt) for dt in out_dtypes]
    out_specs = [pl.BlockSpec((tm, tn), lambda i, j: (i, j)) for _ in range(no)]
    if has_norm:
        out_shape += [jax.ShapeDtypeStruct((m, n_cols), BF16), jax.ShapeDtypeStruct((m, LANES), F32)]
        out_specs += [pl.BlockSpec((tm, tn), lambda i, j: (i, j)),
                      pl.BlockSpec((tm, LANES), lambda i, j: (i, 0))]
    if emit_w:
        for _, w, _, _ in wspecs:
            out_shape.append(jax.ShapeDtypeStruct((1, w.shape[1], n_cols), BF16))
            out_specs.append(pl.BlockSpec((None, w.shape[1], tn), lambda i, j: (0, 0, j)))
    outs = pl.pallas_call(
        kernel,
        out_shape=out_shape,
        grid=(m // tm, n_cols // tn),
        in_specs=in_specs,
        out_specs=out_specs,
        compiler_params=_cparams(2),
        name=name,
    )(*inputs)
    n_norm = 2 if has_norm else 0
    return list(outs[:no]), (tuple(outs[no:no + n_norm]) if has_norm else None), list(outs[no + n_norm:])


def _row_rinv(ssq, n_cols, width):
    return jnp.tile(lax.rsqrt(ssq * (1.0 / n_cols) + EPS), (1, width // LANES))


def _kv_proj_kernel(z_ref, wdq_ref, wkv_ref, gq_ref, gkv_ref, cos_ref, sin_ref,
                    cq_ref, ckv_ref, kpe_ref, k_ref):
    z = z_ref[...]
    a = _dot(z, wdq_ref[...])
    cq = a * lax.rsqrt(jnp.mean(a * a, axis=-1, keepdims=True) + EPS) * gq_ref[...]
    cq_ref[...] = cq.astype(cq_ref.dtype)
    kv = _dot(z, wkv_ref[...])
    c = kv[:, :KV_LORA]
    ckv = c * lax.rsqrt(jnp.mean(c * c, axis=-1, keepdims=True) + EPS) * gkv_ref[...]
    kpe = (kv[:, KV_LORA:KV_LORA + ROPE_PAD] * cos_ref[...]
           + kv[:, KV_LORA + ROPE_PAD:] * sin_ref[...])
    ckv_ref[...] = ckv
    kpe_ref[...] = kpe[:, :ROPE_DIM]
    k_ref[:, :KV_LORA] = ckv.astype(k_ref.dtype)
    k_ref[:, KV_LORA:] = kpe.astype(k_ref.dtype)


def _kv_proj(z, wdq, wkv, gq, gkv, cos, sin, tm=512):
    m = z.shape[0]
    tm = min(tm, m)
    row = lambda i: (i, 0)
    fixed = lambda i: (0, 0)
    return pl.pallas_call(
        _kv_proj_kernel,
        out_shape=[jax.ShapeDtypeStruct((m, Q_LORA), BF16),
                   jax.ShapeDtypeStruct((m, KV_LORA), F32),
                   jax.ShapeDtypeStruct((m, ROPE_DIM), F32),
                   jax.ShapeDtypeStruct((m, QK_PAD), BF16)],
        grid=(m // tm,),
        in_specs=[pl.BlockSpec((tm, D_MODEL), row),
                  pl.BlockSpec(wdq.shape, fixed),
                  pl.BlockSpec(wkv.shape, fixed),
                  pl.BlockSpec((1, Q_LORA), fixed),
                  pl.BlockSpec((1, KV_LORA), fixed),
                  pl.BlockSpec((tm, ROPE_PAD), row),
                  pl.BlockSpec((tm, ROPE_PAD), row)],
        out_specs=[pl.BlockSpec((tm, Q_LORA), row),
                   pl.BlockSpec((tm, KV_LORA), row),
                   pl.BlockSpec((tm, ROPE_DIM), row),
                   pl.BlockSpec((tm, QK_PAD), row)],
        compiler_params=_cparams(1),
        name="kv_proj",
    )(z, wdq, wkv, gq.reshape(1, -1), gkv.reshape(1, -1), cos, sin)


Q_HEADS_PER_STEP = 4


def _q_proj_kernel(cq_ref, wn_ref, wp_ref, wps_ref, wuk_ref, cos_ref, sin_ref, q_ref):
    cq = cq_ref[...]
    qn = _dot(cq, wn_ref[...]).astype(BF16)
    qp = _dot(cq, wp_ref[...])
    qps = _dot(cq, wps_ref[...])
    cos, sin = cos_ref[...], sin_ref[...]
    for h in range(Q_HEADS_PER_STEP):
        ql = _dot(qn[:, h * NOPE_DIM:(h + 1) * NOPE_DIM], wuk_ref[h])
        qr = (qp[:, h * ROPE_PAD:(h + 1) * ROPE_PAD] * cos
              + qps[:, h * ROPE_PAD:(h + 1) * ROPE_PAD] * sin)
        q_ref[:, h * QK_PAD:h * QK_PAD + KV_LORA] = ql.astype(q_ref.dtype)
        q_ref[:, h * QK_PAD + KV_LORA:(h + 1) * QK_PAD] = qr.astype(q_ref.dtype)


def _q_proj(cq, wn, wp, wps, wuk_t, cos, sin, tm):
    m = cq.shape[0]
    tm = min(tm, m)
    hq = Q_HEADS_PER_STEP
    return pl.pallas_call(
        _q_proj_kernel,
        out_shape=jax.ShapeDtypeStruct((m, N_HEADS * QK_PAD), BF16),
        grid=(m // tm, N_HEADS // hq),
        in_specs=[pl.BlockSpec((tm, Q_LORA), lambda i, j: (i, 0)),
                  pl.BlockSpec((Q_LORA, hq * NOPE_DIM), lambda i, j: (0, j)),
                  pl.BlockSpec((Q_LORA, hq * ROPE_PAD), lambda i, j: (0, j)),
                  pl.BlockSpec((Q_LORA, hq * ROPE_PAD), lambda i, j: (0, j)),
                  pl.BlockSpec((hq, NOPE_DIM, KV_LORA), lambda i, j: (j, 0, 0)),
                  pl.BlockSpec((tm, ROPE_PAD), lambda i, j: (i, 0)),
                  pl.BlockSpec((tm, ROPE_PAD), lambda i, j: (i, 0))],
        out_specs=pl.BlockSpec((tm, hq * QK_PAD), lambda i, j: (i, j)),
        compiler_params=_cparams(2),
        name="q_proj",
    )(cq, wn, wp, wps, wuk_t, cos, sin)


def _softmax_update(s, v, m_sc, l_sc, acc_sc):
    n = s.shape[1]
    m_prev = m_sc[...]
    m_new = jnp.maximum(m_prev, jnp.max(s, axis=1, keepdims=True))
    alpha = jnp.exp(m_prev - m_new)
    m_wide = jnp.tile(m_new, (1, n // LANES)) if n % LANES == 0 else m_new[:, :n]
    p = jnp.exp(s - m_wide)
    l_sc[...] = alpha * l_sc[...] + jnp.sum(p, axis=1, keepdims=True)
    acc_sc[...] = jnp.tile(alpha, (1, KV_LORA // LANES)) * acc_sc[...] + _dot(p.astype(BF16), v)
    m_sc[...] = m_new


PROMPT_TQ = 512
PROMPT_HG = 4


def _prompt_attn_kernel(q_ref, k_ref, wuv_ref, o_ref, m_sc, l_sc, acc_sc):
    tq, hg = PROMPT_TQ, PROMPT_HG
    i = pl.program_id(1)
    q = jnp.concatenate([q_ref[:, h * QK_PAD:(h + 1) * QK_PAD] for h in range(hg)], axis=0)
    m_sc[...] = jnp.full(m_sc.shape, -jnp.inf, F32)
    l_sc[...] = jnp.zeros(l_sc.shape, F32)
    acc_sc[...] = jnp.zeros(acc_sc.shape, F32)

    def full_tile(kt, carry):
        k = k_ref[pl.ds(pl.multiple_of(kt * tq, tq), tq), :]
        s = _dot_nt(q, k) * SM_SCALE
        _softmax_update(s, k[:, :KV_LORA], m_sc, l_sc, acc_sc)
        return carry

    lax.fori_loop(0, i, full_tile, 0)

    k = k_ref[pl.ds(pl.multiple_of(i * tq, tq), tq), :]
    s = _dot_nt(q, k) * SM_SCALE
    tok = lax.broadcasted_iota(jnp.int32, s.shape, 0) & (tq - 1)
    key = lax.broadcasted_iota(jnp.int32, s.shape, 1)
    s = jnp.where(key <= tok, s, NEG_BIG)
    _softmax_update(s, k[:, :KV_LORA], m_sc, l_sc, acc_sc)

    o_lat = (acc_sc[...] / jnp.tile(l_sc[...], (1, KV_LORA // LANES))).astype(BF16)
    for h in range(hg):
        o_ref[:, h * V_DIM:(h + 1) * V_DIM] = _dot(
            o_lat[h * tq:(h + 1) * tq, :], wuv_ref[h]).astype(o_ref.dtype)


def _prompt_attn(q, k, wuv_t, batch, seq):
    tq, hg = PROMPT_TQ, PROMPT_HG
    nq = seq // tq
    rows = tq * hg
    return pl.pallas_call(
        _prompt_attn_kernel,
        out_shape=jax.ShapeDtypeStruct((batch * seq, N_HEADS * V_DIM), BF16),
        grid=(batch, nq, N_HEADS // hg),
        in_specs=[pl.BlockSpec((tq, hg * QK_PAD), lambda b, i, g: (b * nq + i, g)),
                  pl.BlockSpec((seq, QK_PAD), lambda b, i, g: (b, 0)),
                  pl.BlockSpec((hg, KV_LORA, V_DIM), lambda b, i, g: (g, 0, 0))],
        out_specs=pl.BlockSpec((tq, hg * V_DIM), lambda b, i, g: (b * nq + i, g)),
        scratch_shapes=[pltpu.VMEM((rows, LANES), F32), pltpu.VMEM((rows, LANES), F32),
                        pltpu.VMEM((rows, KV_LORA), F32)],
        compiler_params=_cparams(3),
        name="prompt_attn",
    )(q, k, wuv_t)


DEC_PAGES = 16
DEC_SLOTS = 3
DEC_NEW_PAD = 16


def _decode_attn_kernel(pt_ref, q_ref, knew_ref, ck_hbm, kp_hbm, o_ref,
                        ck_buf, kp_buf, sem, m_sc, l_sc, acc_sc, *, layer, n_new, n_chunks, page):
    b = pl.program_id(0)
    n_total = pl.num_programs(0) * n_chunks
    ahead = DEC_SLOTS - 1

    def chunk_copies(g, lookup):
        slot = lax.rem(g, DEC_SLOTS)
        if lookup:
            bb = lax.div(g, n_chunks)
            c = g - bb * n_chunks
        copies = []
        for p in range(DEC_PAGES):
            pg = pt_ref[bb, c * DEC_PAGES + p] if lookup else 0
            rows = pl.ds(p * page, page)
            copies.append(pltpu.make_async_copy(
                ck_hbm.at[layer, pg], ck_buf.at[slot, rows, :], sem.at[slot]))
            copies.append(pltpu.make_async_copy(
                kp_hbm.at[layer, pg], kp_buf.at[slot, :, rows], sem.at[slot]))
        return copies

    @pl.when(b == 0)
    def _():
        for g in range(ahead):
            for cp in chunk_copies(jnp.int32(g), True):
                cp.start()

    m_sc[...] = jnp.full(m_sc.shape, -jnp.inf, F32)
    l_sc[...] = jnp.zeros(l_sc.shape, F32)
    acc_sc[...] = jnp.zeros(acc_sc.shape, F32)
    q = q_ref[0]
    q_lat, q_pe = q[:, :KV_LORA], q[:, KV_LORA:KV_LORA + ROPE_DIM]

    def chunk(c, carry):
        g = b * n_chunks + c
        slot = lax.rem(g, DEC_SLOTS)

        @pl.when(g + ahead < n_total)
        def _():
            for cp in chunk_copies(g + ahead, True):
                cp.start()

        for cp in chunk_copies(g, False):
            cp.wait()
        ck = ck_buf[slot].astype(BF16)
        kp_t = kp_buf[slot].astype(BF16)
        s = (_dot_nt(q_lat, ck) + _dot(q_pe, kp_t)) * SM_SCALE
        _softmax_update(s, ck, m_sc, l_sc, acc_sc)
        return carry

    lax.fori_loop(0, n_chunks, chunk, 0)

    kn = knew_ref[0]
    sn = _dot_nt(q, kn) * SM_SCALE
    tok = lax.broadcasted_iota(jnp.int32, sn.shape, 0) >> (N_HEADS.bit_length() - 1)
    key = lax.broadcasted_iota(jnp.int32, sn.shape, 1)
    sn = jnp.where((key <= tok) & (key < n_new), sn, NEG_BIG)
    _softmax_update(sn, kn[:, :KV_LORA], m_sc, l_sc, acc_sc)
    o_ref[0] = (acc_sc[...] / jnp.tile(l_sc[...], (1, KV_LORA // LANES))).astype(o_ref.dtype)


def _decode_attn(q, knew, cache_ckv, cache_kpe_t, page_table, layer, n_new):
    db, rows, _ = q.shape
    n_pages = page_table.shape[1]
    page = cache_ckv.shape[2]
    n_chunks = n_pages // DEC_PAGES
    assert n_pages % DEC_PAGES == 0 and db * n_chunks >= DEC_SLOTS
    keys = DEC_PAGES * page
    return pl.pallas_call(
        functools.partial(_decode_attn_kernel, layer=layer, n_new=n_new, n_chunks=n_chunks, page=page),
        out_shape=jax.ShapeDtypeStruct((db, rows, KV_LORA), BF16),
        grid_spec=pltpu.PrefetchScalarGridSpec(
            num_scalar_prefetch=1,
            grid=(db,),
            in_specs=[pl.BlockSpec((1, rows, QK_PAD), lambda b, pt: (b, 0, 0)),
                      pl.BlockSpec((1, DEC_NEW_PAD, QK_PAD), lambda b, pt: (b, 0, 0)),
                      pl.BlockSpec(memory_space=pl.ANY),
                      pl.BlockSpec(memory_space=pl.ANY)],
            out_specs=pl.BlockSpec((1, rows, KV_LORA), lambda b, pt: (b, 0, 0)),
            scratch_shapes=[pltpu.VMEM((DEC_SLOTS, keys, KV_LORA), F32),
                            pltpu.VMEM((DEC_SLOTS, ROPE_DIM, keys), F32),
                            pltpu.SemaphoreType.DMA((DEC_SLOTS,)),
                            pltpu.VMEM((rows, LANES), F32), pltpu.VMEM((rows, LANES), F32),
                            pltpu.VMEM((rows, KV_LORA), F32)]),
        compiler_params=_cparams(1),
        name="decode_attn",
    )(page_table, q, knew, cache_ckv, cache_kpe_t)


UV_HEADS_PER_STEP = 4


def _uv_proj_kernel(x_ref, w_ref, o_ref):
    for h in range(UV_HEADS_PER_STEP):
        o_ref[:, h * V_DIM:(h + 1) * V_DIM] = _dot(
            x_ref[:, h * KV_LORA:(h + 1) * KV_LORA], w_ref[h]).astype(o_ref.dtype)


def _uv_proj(o_lat, wuv_t, tm):
    m = o_lat.shape[0]
    hu = UV_HEADS_PER_STEP
    return pl.pallas_call(
        _uv_proj_kernel,
        out_shape=jax.ShapeDtypeStruct((m, N_HEADS * V_DIM), BF16),
        grid=(m // tm, N_HEADS // hu),
        in_specs=[pl.BlockSpec((tm, hu * KV_LORA), lambda i, j: (i, j)),
                  pl.BlockSpec((hu, KV_LORA, V_DIM), lambda i, j: (j, 0, 0))],
        out_specs=pl.BlockSpec((tm, hu * V_DIM), lambda i, j: (i, j)),
        compiler_params=_cparams(2),
        name="uv_proj",
    )(o_lat, wuv_t)


CONV_TS = 256
CONV_HALO = 32
CONV_RC = 128
CONV_LN_ROWS = 64
SUBLANES = 8


def _ln_silu(y, g, b):
    yc = y - jnp.mean(y, axis=-1, keepdims=True)
    yn = yc * lax.rsqrt(jnp.mean(yc * yc, axis=-1, keepdims=True) + EPS) * g + b
    return yn * jax.nn.sigmoid(yn)


def _conv_prompt_kernel(halo_ref, x_ref, w_ref, bdw_ref, g_ref, b_ref, o_ref, xw_sc, y_sc, sh_sc):
    ts = CONV_TS
    i = pl.program_id(1)
    xw_sc[0:CONV_HALO, :] = jnp.where(i > 0, halo_ref[...], 0.0)
    xw_sc[CONV_HALO:CONV_HALO + ts, :] = x_ref[...]
    base = CONV_HALO - (CONV_W - 1)

    def col_chunk(c, carry):
        col = pl.multiple_of(c * LANES, LANES)
        for rc in range(ts // CONV_RC):
            row0 = rc * CONV_RC
            acc = jnp.zeros((CONV_RC, LANES), F32)
            for r in range(SUBLANES):
                taps = [(o // SUBLANES, o - base) for o in range(r, CONV_HALO + 1, SUBLANES)
                        if 0 <= o - base < CONV_W]
                j_lo, j_hi = taps[0][0], taps[-1][0]
                start = row0 + r + j_lo * SUBLANES
                n_rows = (j_hi - j_lo) * SUBLANES + CONV_RC
                sh_sc[r, 0:n_rows, :] = xw_sc[start:start + n_rows, pl.ds(col, LANES)]
                for j, k in taps:
                    off = (j - j_lo) * SUBLANES
                    acc = acc + w_ref[k:k + 1, pl.ds(col, LANES)] * sh_sc[r, off:off + CONV_RC, :]
            y_sc[row0:row0 + CONV_RC, pl.ds(col, LANES)] = acc + bdw_ref[:, pl.ds(col, LANES)]
        return carry

    lax.fori_loop(0, D_MODEL // LANES, col_chunk, 0)

    def ln_chunk(r, carry):
        rs = pl.ds(pl.multiple_of(r * CONV_LN_ROWS, CONV_LN_ROWS), CONV_LN_ROWS)
        o_ref[rs, :] = _ln_silu(y_sc[rs, :], g_ref[...], b_ref[...]).astype(o_ref.dtype)
        return carry

    lax.fori_loop(0, ts // CONV_LN_ROWS, ln_chunk, 0)


def _conv_prompt(u, w_dw, b_dw, g_ln, b_ln, batch, seq):
    ts = CONV_TS
    nt = seq // ts
    hb = ts // CONV_HALO
    fixed = lambda b, i: (0, 0)
    return pl.pallas_call(
        _conv_prompt_kernel,
        out_shape=jax.ShapeDtypeStruct((batch * seq, D_MODEL), BF16),
        grid=(batch, nt),
        in_specs=[pl.BlockSpec((CONV_HALO, D_MODEL),
                               lambda b, i: (jnp.maximum((b * nt + i) * hb - 1, 0), 0)),
                  pl.BlockSpec((ts, D_MODEL), lambda b, i: (b * nt + i, 0)),
                  pl.BlockSpec((CONV_W, D_MODEL), fixed),
                  pl.BlockSpec((1, D_MODEL), fixed),
                  pl.BlockSpec((1, D_MODEL), fixed),
                  pl.BlockSpec((1, D_MODEL), fixed)],
        out_specs=pl.BlockSpec((ts, D_MODEL), lambda b, i: (b * nt + i, 0)),
        scratch_shapes=[pltpu.VMEM((CONV_HALO + ts, D_MODEL), F32),
                        pltpu.VMEM((ts, D_MODEL), F32),
                        pltpu.VMEM((SUBLANES, CONV_HALO + CONV_RC, LANES), F32)],
        compiler_params=_cparams(2),
        name="conv_prompt",
    )(u, u, w_dw, b_dw.reshape(1, -1), g_ln.reshape(1, -1), b_ln.reshape(1, -1))


CONV_SB = 8


def _conv_sample_kernel(st_ref, u_ref, wt_ref, bdw_ref, g_ref, b_ref, o_ref, ns_ref, full_sc):
    n_prev = CONV_W - 1
    t = u_ref.shape[1]
    rows = full_sc.shape[0]
    full_sc[n_prev + t:rows, :] = jnp.zeros((rows - n_prev - t, D_MODEL), F32)

    def one(bi, carry):
        full_sc[0:n_prev, :] = st_ref[bi]
        full_sc[n_prev:n_prev + t, :] = u_ref[bi]
        full = full_sc[...]
        y = jnp.concatenate([jnp.sum(full * wt_ref[tt], axis=0, keepdims=True) for tt in range(t)],
                            axis=0)
        o_ref[bi] = _ln_silu(y + bdw_ref[...], g_ref[...], b_ref[...]).astype(o_ref.dtype)
        ns_ref[bi] = full_sc[t:t + n_prev, :]
        return carry

    lax.fori_loop(0, CONV_SB, one, 0)


def _conv_sample(state, u, w_dw, b_dw, g_ln, b_ln):
    db, n_prev, d = state.shape
    t = u.shape[1]
    sb = CONV_SB
    rows = -(-(n_prev + t) // SUBLANES) * SUBLANES
    w_shift = jnp.stack([jnp.pad(w_dw, ((tt, rows - CONV_W - tt), (0, 0))) for tt in range(t)])
    fixed = lambda i: (0, 0)
    blk = lambda i: (i, 0, 0)
    return pl.pallas_call(
        _conv_sample_kernel,
        out_shape=[jax.ShapeDtypeStruct((db, t, d), F32),
                   jax.ShapeDtypeStruct((db, n_prev, d), F32)],
        grid=(db // sb,),
        in_specs=[pl.BlockSpec((sb, n_prev, d), blk),
                  pl.BlockSpec((sb, t, d), blk),
                  pl.BlockSpec((t, rows, d), lambda i: (0, 0, 0)),
                  pl.BlockSpec((1, d), fixed),
                  pl.BlockSpec((1, d), fixed),
                  pl.BlockSpec((1, d), fixed)],
        out_specs=[pl.BlockSpec((sb, t, d), blk), pl.BlockSpec((sb, n_prev, d), blk)],
        scratch_shapes=[pltpu.VMEM((rows, d), F32)],
        compiler_params=_cparams(1),
        name="conv_sample",
    )(state, u, w_shift, b_dw.reshape(1, -1), g_ln.reshape(1, -1), b_ln.reshape(1, -1))


TM_PROMPT = 1024


def _residual_add(accs, ex):
    out = ex[0] + accs[0]
    for e in ex[1:]:
        out = out + e
    return [out]


_mm = _matmul


def _normed(h, pair, g):
    return pair if pair is not None else (_rmsnorm(h, g, BF16), None)


def _row_extras(stat):
    return [] if stat is None else [(stat, "row")]


def _scaled(accs, ex, has_stat, which):
    if not has_stat:
        return accs, ex
    r = _row_rinv(ex[0], D_MODEL, accs[0].shape[1])
    return [a * r if k in which else a for k, a in enumerate(accs)], ex[1:]


def _linear_residual(xs, xp, hs, hp, w, layer, name, g_next, bias=None):
    ms = xs.shape[0]
    cols = [] if bias is None else [(bias.reshape(1, -1), "col")]
    (hs2,), ns, (wb,) = _mm([xs], [(0, w, layer, 0)], [(hs, "tile")] + cols, _residual_add, [F32],
                            n_cols=D_MODEL, tm=ms, tn=256, name=name, emit_w=True, norm_gain=g_next)
    (hp2,), np_, _ = _mm([xp], [(0, wb, 0, 0)], [(hp, "tile")] + cols, _residual_add, [F32],
                         n_cols=D_MODEL, tm=TM_PROMPT, tn=512, name=name, norm_gain=g_next)
    return hs2, ns, hp2, np_


def _ffn(hs, ns, hp, np_, g, wg, wu, wd, layer, g_next):
    ms = hs.shape[0]
    d_ff = wg.shape[2]
    (zs, ss), (zp, sp) = _normed(hs, ns, g), _normed(hp, np_, g)

    def gate_up(has_stat):
        def epilogue(accs, ex):
            (a, b), _ = _scaled(accs, ex, has_stat, (0, 1))
            return [jax.nn.silu(a) * b]
        return epilogue

    (act_s,), _, (wgb, wub) = _mm([zs], [(0, wg, layer, 0), (0, wu, layer, 0)], _row_extras(ss),
                                  gate_up(ss is not None), [BF16],
                                  n_cols=d_ff, tm=ms, tn=256, name="ffn_gate_up", emit_w=True)
    (act_p,), _, _ = _mm([zp], [(0, wgb, 0, 0), (0, wub, 0, 0)], _row_extras(sp),
                         gate_up(sp is not None), [BF16],
                         n_cols=d_ff, tm=TM_PROMPT, tn=256, name="ffn_gate_up")
    (hs2,), ns2, (wdb,) = _mm([act_s], [(0, wd, layer, 0)], [(hs, "tile")], _residual_add, [F32],
                              n_cols=D_MODEL, tm=ms, tn=128, name="ffn_down", emit_w=True,
                              norm_gain=g_next)
    (hp2,), np2, _ = _mm([act_p], [(0, wdb, 0, 0)], [(hp, "tile")], _residual_add, [F32],
                         n_cols=D_MODEL, tm=512, tn=512, name="ffn_down", norm_gain=g_next)
    return hs2, ns2, hp2, np2


def _ple(hs, ns, hp, np_, ps, pp, g, w_gate, w_up, layer, g_next):
    ms = hs.shape[0]
    (zs, ss), (zp, sp) = _normed(hs, ns, g), _normed(hp, np_, g)

    def gated(has_stat):
        def epilogue(accs, ex):
            (a, b), ex = _scaled(accs, ex, has_stat, (0,))
            return [ex[0] + jax.nn.sigmoid(a) * b]
        return epilogue

    (hs2,), ns2, (wgb, wub) = _mm([zs, ps], [(0, w_gate, layer, 0), (1, w_up, layer, 0)],
                                  _row_extras(ss) + [(hs, "tile")], gated(ss is not None), [F32],
                                  n_cols=D_MODEL, tm=ms, tn=256, name="ple", emit_w=True,
                                  norm_gain=g_next)
    (hp2,), np2, _ = _mm([zp, pp], [(0, wgb, 0, 0), (1, wub, 0, 0)],
                         _row_extras(sp) + [(hp, "tile")], gated(sp is not None), [F32],
                         n_cols=D_MODEL, tm=TM_PROMPT, tn=512, name="ple", norm_gain=g_next)
    return hs2, ns2, hp2, np2


def _rope_tables(pos):
    half = ROPE_DIM // 2
    inv_freq = ROPE_THETA ** (-jnp.arange(half, dtype=F32) / half)
    ang = pos.astype(F32)[:, None] * inv_freq
    cos, sin = jnp.cos(ang), jnp.sin(ang)
    pad = jnp.zeros((pos.shape[0], ROPE_PAD - ROPE_DIM), F32)
    return (jnp.concatenate([cos, cos, pad], axis=1),
            jnp.concatenate([-sin, sin, pad], axis=1))


def _swap_halves(w):
    half = ROPE_DIM // 2
    return jnp.concatenate([w[..., half:], w[..., :half]], axis=-1)


def _pad_rope(w):
    return jnp.concatenate([w, jnp.zeros(w.shape[:-1] + (ROPE_PAD - ROPE_DIM,), w.dtype)], axis=-1)


def kernel(x_prompt, x_sample, cache_ckv, cache_kpe, state_conv, page_table, p_prompt, p_sample,
           g_attn_norm, w_dq, g_q, w_uq, w_dkv, g_kv, w_uk, w_uv, w_o,
           g_conv_norm, w_pw1, b_pw1, w_dw, b_dw, g_conv_ln, b_conv_ln, w_pw2, b_pw2,
           g_ffn_norm, w_ffn_gate, w_ffn_up, w_ffn_down,
           g_ple_norm, w_ple_gate, w_ple_up, g_final):
    B, S, _ = x_prompt.shape
    DB, T, _ = x_sample.shape
    depth = g_ffn_norm.shape[0]
    past = page_table.shape[1] * cache_ckv.shape[2]
    MP, MS = B * S, DB * T

    cos_p, sin_p = _rope_tables(jnp.tile(jnp.arange(S, dtype=jnp.int32), B))
    cos_s, sin_s = _rope_tables(jnp.tile(past + jnp.arange(T, dtype=jnp.int32), DB))

    hp = x_prompt.reshape(MP, D_MODEL)
    hs = x_sample.reshape(MS, D_MODEL)
    ckv_p, kpe_p, conv_p, ckv_s, kpe_s, conv_s = [], [], [], [], [], []
    ns = np_ = None

    for i in range(depth):
        j = i // 2
        if i % 2 == 0:
            wdq = w_dq[j].astype(BF16)
            w_kpe = w_dkv[j][:, KV_LORA:]
            wkv = jnp.concatenate([w_dkv[j][:, :KV_LORA], _pad_rope(w_kpe),
                                   _pad_rope(_swap_halves(w_kpe))], axis=1).astype(BF16)
            wq3 = w_uq[j].reshape(Q_LORA, N_HEADS, QK_DIM)
            wn = wq3[:, :, :NOPE_DIM].reshape(Q_LORA, N_HEADS * NOPE_DIM).astype(BF16)
            wp = _pad_rope(wq3[:, :, NOPE_DIM:]).reshape(Q_LORA, N_HEADS * ROPE_PAD).astype(BF16)
            wps = _pad_rope(_swap_halves(wq3[:, :, NOPE_DIM:])).reshape(
                Q_LORA, N_HEADS * ROPE_PAD).astype(BF16)
            wuk_t = jnp.transpose(w_uk[j], (1, 2, 0)).astype(BF16)
            wuv_t = jnp.transpose(w_uv[j], (1, 0, 2)).astype(BF16)

            z = _rmsnorm(hs, g_attn_norm[j], BF16)
            cq, ck, kp, kk = _kv_proj(z, wdq, wkv, g_q[j], g_kv[j], cos_s, sin_s)
            q = _q_proj(cq, wn, wp, wps, wuk_t, cos_s, sin_s, tm=MS)
            knew = jnp.pad(kk.reshape(DB, T, QK_PAD), ((0, 0), (0, DEC_NEW_PAD - T), (0, 0)))
            o_lat = _decode_attn(q.reshape(DB, T * N_HEADS, QK_PAD), knew,
                                 cache_ckv, jnp.swapaxes(cache_kpe, 2, 3), page_table, j, T)
            o_s = _uv_proj(o_lat.reshape(MS, N_HEADS * KV_LORA), wuv_t, tm=MS)
            ckv_s.append(ck.reshape(DB, T, KV_LORA))
            kpe_s.append(kp.reshape(DB, T, ROPE_DIM))

            z = _rmsnorm(hp, g_attn_norm[j], BF16)
            cq, ck, kp, kk = _kv_proj(z, wdq, wkv, g_q[j], g_kv[j], cos_p, sin_p)
            q = _q_proj(cq, wn, wp, wps, wuk_t, cos_p, sin_p, tm=TM_PROMPT)
            o_p = _prompt_attn(q, kk, wuv_t, B, S)
            ckv_p.append(ck.reshape(B, S, KV_LORA))
            kpe_p.append(kp.reshape(B, S, ROPE_DIM))

            hs, ns, hp, np_ = _linear_residual(o_s, o_p, hs, hp, w_o, j, "attn_out", g_ffn_norm[i])
        else:
            b1 = b_pw1[j].reshape(1, -1)
            biases = [(b1[:, :D_MODEL], "col"), (b1[:, D_MODEL:], "col")]
            (zs, ss), (zp, sp) = _normed(hs, ns, g_conv_norm[j]), _normed(hp, np_, g_conv_norm[j])

            def glu(has_stat):
                def epilogue(accs, ex):
                    (a, b), ex = _scaled(accs, ex, has_stat, (0, 1))
                    return [(a + ex[0]) * jax.nn.sigmoid(b + ex[1])]
                return epilogue

            (u_s,), _, (w1a, w1b) = _mm([zs], [(0, w_pw1, j, 0), (0, w_pw1, j, D_MODEL // 256)],
                                        _row_extras(ss) + biases, glu(ss is not None), [F32],
                                        n_cols=D_MODEL, tm=MS, tn=256, name="pw1_glu", emit_w=True)
            (u_p,), _, _ = _mm([zp], [(0, w1a, 0, 0), (0, w1b, 0, 0)], _row_extras(sp) + biases,
                               glu(sp is not None), [F32],
                               n_cols=D_MODEL, tm=TM_PROMPT, tn=512, name="pw1_glu")
            c_s, new_state = _conv_sample(state_conv[j], u_s.reshape(DB, T, D_MODEL),
                                          w_dw[j], b_dw[j], g_conv_ln[j], b_conv_ln[j])
            c_p = _conv_prompt(u_p, w_dw[j], b_dw[j], g_conv_ln[j], b_conv_ln[j], B, S)
            conv_s.append(new_state)
            conv_p.append(u_p.reshape(B, S, D_MODEL)[:, S - (CONV_W - 1):])
            hs, ns, hp, np_ = _linear_residual(c_s.reshape(MS, D_MODEL), c_p, hs, hp, w_pw2, j, "pw2",
                                               g_ffn_norm[i], bias=b_pw2[j])

        hs, ns, hp, np_ = _ffn(hs, ns, hp, np_, g_ffn_norm[i], w_ffn_gate, w_ffn_up, w_ffn_down, i,
                               g_ple_norm[i])
        next_is_conv = i + 1 < depth and (i + 1) % 2 == 1
        hs, ns, hp, np_ = _ple(hs, ns, hp, np_, p_sample[i].reshape(MS, -1), p_prompt[i].reshape(MP, -1),
                               g_ple_norm[i], w_ple_gate, w_ple_up, i,
                               g_conv_norm[(i + 1) // 2] if next_is_conv else None)

    y_prompt = _rmsnorm(hp, g_final, F32).reshape(B, S, D_MODEL)
    y_sample = _rmsnorm(hs, g_final, F32).reshape(DB, T, D_MODEL)
    return (y_prompt, y_sample,
            jnp.stack(ckv_p), jnp.stack(kpe_p), jnp.stack(conv_p),
            jnp.stack(ckv_s), jnp.stack(kpe_s), jnp.stack(conv_s))
```

```python
import functools
import math

import jax
import jax.numpy as jnp
from jax import lax
from jax.experimental import pallas as pl
from jax.experimental.pallas import tpu as pltpu

D_MODEL = 4096
N_HEADS = 32
Q_LORA = 1024
KV_LORA = 512
NOPE_DIM = 128
ROPE_DIM = 64
V_DIM = 128
QK_DIM = NOPE_DIM + ROPE_DIM
SM_SCALE = 1.0 / math.sqrt(QK_DIM)
ROPE_THETA = 10000.0
CONV_W = 31
EPS = 1e-6

LANES = 128
ROPE_PAD = LANES
QK_PAD = KV_LORA + ROPE_PAD
NEG_BIG = -1e30
VMEM_LIMIT = 56 * 1024 * 1024

F32 = jnp.float32
BF16 = jnp.bfloat16


def _cparams(n_axes):
    return pltpu.CompilerParams(dimension_semantics=("arbitrary",) * n_axes,
                                vmem_limit_bytes=VMEM_LIMIT)


def _dot(a, b):
    return jnp.dot(a, b, preferred_element_type=F32)


def _dot_nt(a, b):
    return lax.dot_general(a, b, (((1,), (1,)), ((), ())), preferred_element_type=F32)


def _rmsnorm_kernel(x_ref, g_ref, o_ref):
    x = x_ref[...]
    y = x * lax.rsqrt(jnp.mean(x * x, axis=-1, keepdims=True) + EPS)
    o_ref[...] = (y * g_ref[...]).astype(o_ref.dtype)


def _rmsnorm(x, g, out_dtype, tm=256):
    m, d = x.shape
    tm = min(tm, m)
    return pl.pallas_call(
        _rmsnorm_kernel,
        out_shape=jax.ShapeDtypeStruct((m, d), out_dtype),
        grid=(m // tm,),
        in_specs=[pl.BlockSpec((tm, d), lambda i: (i, 0)),
                  pl.BlockSpec((1, d), lambda i: (0, 0))],
        out_specs=pl.BlockSpec((tm, d), lambda i: (i, 0)),
        compiler_params=_cparams(1),
        name="rmsnorm",
    )(x, g.reshape(1, d))


def _matmul(xs, wspecs, extras, epilogue, out_dtypes, *, n_cols, tm, tn, name,
            emit_w=False, norm_gain=None):
    m = xs[0].shape[0]
    tm = min(tm, m)
    nx, nw, ne, no = len(xs), len(wspecs), len(extras), len(out_dtypes)
    has_norm = norm_gain is not None
    assert not emit_w or m == tm
    assert not has_norm or out_dtypes[0] == F32
    in_specs = [pl.BlockSpec((tm, x.shape[1]), lambda i, j: (i, 0)) for x in xs]
    for _, w, layer, off in wspecs:
        in_specs.append(pl.BlockSpec((None, w.shape[1], tn),
                                     lambda i, j, layer=layer, off=off: (layer, 0, j + off)))
    for _, kind in extras:
        if kind == "tile":
            in_specs.append(pl.BlockSpec((tm, tn), lambda i, j: (i, j)))
        elif kind == "row":
            in_specs.append(pl.BlockSpec((tm, LANES), lambda i, j: (i, 0)))
        else:
            in_specs.append(pl.BlockSpec((1, tn), lambda i, j: (0, j)))
    inputs = [*xs, *[w for _, w, _, _ in wspecs], *[a for a, _ in extras]]
    if has_norm:
        in_specs.append(pl.BlockSpec((1, tn), lambda i, j: (0, j)))
        inputs.append(norm_gain.reshape(1, n_cols))
    x_index = [xi for xi, _, _, _ in wspecs]
    n_in = nx + nw + ne + has_norm

    def kernel(*refs):
        x_refs, w_refs = refs[:nx], refs[nx:nx + nw]
        e_refs = refs[nx + nw:nx + nw + ne]
        o_refs = refs[n_in:n_in + no]
        rest = refs[n_in + no:]
        xv = [r[...].astype(BF16) for r in x_refs]
        wv = [r[...].astype(BF16) for r in w_refs]
        accs = [_dot(xv[xi], w) for xi, w in zip(x_index, wv)]
        outs = epilogue(accs, [e[...] for e in e_refs])
        for o_ref, o in zip(o_refs, outs):
            o_ref[...] = o.astype(o_ref.dtype)
        if has_norm:
            hb_ref, ssq_ref = rest[:2]
            rest = rest[2:]
            h = outs[0]
            hb_ref[...] = (h * refs[n_in - 1][...]).astype(BF16)
            part = jnp.broadcast_to(jnp.sum(h * h, axis=1, keepdims=True), ssq_ref.shape)

            @pl.when(pl.program_id(1) == 0)
            def _():
                ssq_ref[...] = part

            @pl.when(pl.program_id(1) > 0)
            def _():
                ssq_ref[...] += part

            @pl.when(pl.program_id(1) == pl.num_programs(1) - 1)
            def _():
                ssq_ref[...] = lax.rsqrt(ssq_ref[...] * (1.0 / n_cols) + EPS)
        for wo_ref, w in zip(rest, wv):
            wo_ref[...] = w

    out_shape = [jax.ShapeDtypeStruct((m, n_cols), dt) for dt in out_dtypes]
    out_specs = [pl.BlockSpec((tm, tn), lambda i, j: (i, j)) for _ in range(no)]
    if has_norm:
        out_shape += [jax.ShapeDtypeStruct((m, n_cols), BF16), jax.ShapeDtypeStruct((m, LANES), F32)]
        out_specs += [pl.BlockSpec((tm, tn), lambda i, j: (i, j)),
                      pl.BlockSpec((tm, LANES), lambda i, j: (i, 0))]
    if emit_w:
        for _, w, _, _ in wspecs:
            out_shape.append(jax.ShapeDtypeStruct((1, w.shape[1], n_cols), BF16))
            out_specs.append(pl.BlockSpec((None, w.shape[1], tn), lambda i, j: (0, 0, j)))
    outs = pl.pallas_call(
        kernel,
        out_shape=out_shape,
        grid=(m // tm, n_cols // tn),
        in_specs=in_specs,
        out_specs=out_specs,
        compiler_params=_cparams(2),
        name=name,
    )(*inputs)
    n_norm = 2 if has_norm else 0
    return list(outs[:no]), (tuple(outs[no:no + n_norm]) if has_norm else None), list(outs[no + n_norm:])


def _row_rinv(rinv, width):
    return jnp.tile(rinv, (1, width // LANES))


def _kv_proj_kernel(z_ref, wdq_ref, wkv_ref, gq_ref, gkv_ref, cos_ref, sin_ref,
                    cq_ref, ckv_ref, kpe_ref, k_ref):
    z = z_ref[...]
    a = _dot(z, wdq_ref[...])
    cq = a * lax.rsqrt(jnp.mean(a * a, axis=-1, keepdims=True) + EPS) * gq_ref[...]
    cq_ref[...] = cq.astype(cq_ref.dtype)
    kv = _dot(z, wkv_ref[...])
    c = kv[:, :KV_LORA]
    ckv = c * lax.rsqrt(jnp.mean(c * c, axis=-1, keepdims=True) + EPS) * gkv_ref[...]
    kpe = (kv[:, KV_LORA:KV_LORA + ROPE_PAD] * cos_ref[...]
           + kv[:, KV_LORA + ROPE_PAD:] * sin_ref[...])
    ckv_ref[...] = ckv
    kpe_ref[...] = kpe[:, :ROPE_DIM]
    k_ref[:, :KV_LORA] = ckv.astype(k_ref.dtype)
    k_ref[:, KV_LORA:] = kpe.astype(k_ref.dtype)


def _kv_proj(z, wdq, wkv, gq, gkv, cos, sin, tm=512):
    m = z.shape[0]
    tm = min(tm, m)
    row = lambda i: (i, 0)
    fixed = lambda i: (0, 0)
    return pl.pallas_call(
        _kv_proj_kernel,
        out_shape=[jax.ShapeDtypeStruct((m, Q_LORA), BF16),
                   jax.ShapeDtypeStruct((m, KV_LORA), F32),
                   jax.ShapeDtypeStruct((m, ROPE_DIM), F32),
                   jax.ShapeDtypeStruct((m, QK_PAD), BF16)],
        grid=(m // tm,),
        in_specs=[pl.BlockSpec((tm, D_MODEL), row),
                  pl.BlockSpec(wdq.shape, fixed),
                  pl.BlockSpec(wkv.shape, fixed),
                  pl.BlockSpec((1, Q_LORA), fixed),
                  pl.BlockSpec((1, KV_LORA), fixed),
                  pl.BlockSpec((tm, ROPE_PAD), row),
                  pl.BlockSpec((tm, ROPE_PAD), row)],
        out_specs=[pl.BlockSpec((tm, Q_LORA), row),
                   pl.BlockSpec((tm, KV_LORA), row),
                   pl.BlockSpec((tm, ROPE_DIM), row),
                   pl.BlockSpec((tm, QK_PAD), row)],
        compiler_params=_cparams(1),
        name="kv_proj",
    )(z, wdq, wkv, gq.reshape(1, -1), gkv.reshape(1, -1), cos, sin)


Q_HEADS_PER_STEP = 4


def _q_proj_kernel(cq_ref, wn_ref, wp_ref, wps_ref, wuk_ref, cos_ref, sin_ref, q_ref):
    cq = cq_ref[...]
    qn = _dot(cq, wn_ref[...]).astype(BF16)
    qp = _dot(cq, wp_ref[...])
    qps = _dot(cq, wps_ref[...])
    cos, sin = cos_ref[...], sin_ref[...]
    for h in range(Q_HEADS_PER_STEP):
        ql = _dot(qn[:, h * NOPE_DIM:(h + 1) * NOPE_DIM], wuk_ref[h])
        qr = (qp[:, h * ROPE_PAD:(h + 1) * ROPE_PAD] * cos
              + qps[:, h * ROPE_PAD:(h + 1) * ROPE_PAD] * sin)
        q_ref[:, h * QK_PAD:h * QK_PAD + KV_LORA] = ql.astype(q_ref.dtype)
        q_ref[:, h * QK_PAD + KV_LORA:(h + 1) * QK_PAD] = qr.astype(q_ref.dtype)


def _q_proj(cq, wn, wp, wps, wuk_t, cos, sin, tm):
    m = cq.shape[0]
    tm = min(tm, m)
    hq = Q_HEADS_PER_STEP
    return pl.pallas_call(
        _q_proj_kernel,
        out_shape=jax.ShapeDtypeStruct((m, N_HEADS * QK_PAD), BF16),
        grid=(m // tm, N_HEADS // hq),
        in_specs=[pl.BlockSpec((tm, Q_LORA), lambda i, j: (i, 0)),
                  pl.BlockSpec((Q_LORA, hq * NOPE_DIM), lambda i, j: (0, j)),
                  pl.BlockSpec((Q_LORA, hq * ROPE_PAD), lambda i, j: (0, j)),
                  pl.BlockSpec((Q_LORA, hq * ROPE_PAD), lambda i, j: (0, j)),
                  pl.BlockSpec((hq, NOPE_DIM, KV_LORA), lambda i, j: (j, 0, 0)),
                  pl.BlockSpec((tm, ROPE_PAD), lambda i, j: (i, 0)),
                  pl.BlockSpec((tm, ROPE_PAD), lambda i, j: (i, 0))],
        out_specs=pl.BlockSpec((tm, hq * QK_PAD), lambda i, j: (i, j)),
        compiler_params=_cparams(2),
        name="q_proj",
    )(cq, wn, wp, wps, wuk_t, cos, sin)


def _softmax_update(s, v, m_sc, l_sc, acc_sc):
    n = s.shape[1]
    m_prev = m_sc[...]
    m_new = jnp.maximum(m_prev, jnp.max(s, axis=1, keepdims=True))
    alpha = jnp.exp(m_prev - m_new)
    m_wide = jnp.tile(m_new, (1, n // LANES)) if n % LANES == 0 else m_new[:, :n]
    p = jnp.exp(s - m_wide)
    l_sc[...] = alpha * l_sc[...] + jnp.sum(p, axis=1, keepdims=True)
    acc_sc[...] = jnp.tile(alpha, (1, KV_LORA // LANES)) * acc_sc[...] + _dot(p.astype(BF16), v)
    m_sc[...] = m_new


PROMPT_TQ = 512
PROMPT_HG = 4


def _prompt_attn_kernel(q_ref, k_ref, wuv_ref, o_ref, m_sc, l_sc, acc_sc):
    tq, hg = PROMPT_TQ, PROMPT_HG
    i = pl.program_id(1)
    q = jnp.concatenate([q_ref[:, h * QK_PAD:(h + 1) * QK_PAD] for h in range(hg)], axis=0)
    m_sc[...] = jnp.full(m_sc.shape, -jnp.inf, F32)
    l_sc[...] = jnp.zeros(l_sc.shape, F32)
    acc_sc[...] = jnp.zeros(acc_sc.shape, F32)

    def full_tile(kt, carry):
        k = k_ref[pl.ds(pl.multiple_of(kt * tq, tq), tq), :]
        s = _dot_nt(q, k) * SM_SCALE
        _softmax_update(s, k[:, :KV_LORA], m_sc, l_sc, acc_sc)
        return carry

    lax.fori_loop(0, i, full_tile, 0)

    k = k_ref[pl.ds(pl.multiple_of(i * tq, tq), tq), :]
    s = _dot_nt(q, k) * SM_SCALE
    tok = lax.broadcasted_iota(jnp.int32, s.shape, 0) & (tq - 1)
    key = lax.broadcasted_iota(jnp.int32, s.shape, 1)
    s = jnp.where(key <= tok, s, NEG_BIG)
    _softmax_update(s, k[:, :KV_LORA], m_sc, l_sc, acc_sc)

    o_lat = (acc_sc[...] / jnp.tile(l_sc[...], (1, KV_LORA // LANES))).astype(BF16)
    for h in range(hg):
        o_ref[:, h * V_DIM:(h + 1) * V_DIM] = _dot(
            o_lat[h * tq:(h + 1) * tq, :], wuv_ref[h]).astype(o_ref.dtype)


def _prompt_attn(q, k, wuv_t, batch, seq):
    tq, hg = PROMPT_TQ, PROMPT_HG
    nq = seq // tq
    rows = tq * hg
    return pl.pallas_call(
        _prompt_attn_kernel,
        out_shape=jax.ShapeDtypeStruct((batch * seq, N_HEADS * V_DIM), BF16),
        grid=(batch, nq, N_HEADS // hg),
        in_specs=[pl.BlockSpec((tq, hg * QK_PAD), lambda b, i, g: (b * nq + i, g)),
                  pl.BlockSpec((seq, QK_PAD), lambda b, i, g: (b, 0)),
                  pl.BlockSpec((hg, KV_LORA, V_DIM), lambda b, i, g: (g, 0, 0))],
        out_specs=pl.BlockSpec((tq, hg * V_DIM), lambda b, i, g: (b * nq + i, g)),
        scratch_shapes=[pltpu.VMEM((rows, LANES), F32), pltpu.VMEM((rows, LANES), F32),
                        pltpu.VMEM((rows, KV_LORA), F32)],
        compiler_params=_cparams(3),
        name="prompt_attn",
    )(q, k, wuv_t)


DEC_PAGES = 16
DEC_SLOTS = 3
DEC_NEW_PAD = 16


def _decode_attn_kernel(pt_ref, q_ref, knew_ref, ck_hbm, kp_hbm, o_ref,
                        ck_buf, kp_buf, sem, s_sc, kb_sc, m_sc, l_sc, acc_sc,
                        *, layer, n_new, n_chunks, page):
    b = pl.program_id(0)
    n_total = pl.num_programs(0) * n_chunks
    ahead = DEC_SLOTS - 1

    def chunk_copies(g, lookup):
        slot = lax.rem(g, DEC_SLOTS)
        if lookup:
            bb = lax.div(g, n_chunks)
            c = g - bb * n_chunks
        copies = []
        for p in range(DEC_PAGES):
            pg = pt_ref[bb, c * DEC_PAGES + p] if lookup else 0
            rows = pl.ds(p * page, page)
            copies.append(pltpu.make_async_copy(
                ck_hbm.at[layer, pg], ck_buf.at[slot, rows, :], sem.at[slot]))
            copies.append(pltpu.make_async_copy(
                kp_hbm.at[layer, pg], kp_buf.at[slot, :, rows], sem.at[slot]))
        return copies

    @pl.when(b == 0)
    def _():
        for g in range(ahead):
            for cp in chunk_copies(jnp.int32(g), True):
                cp.start()

    m_sc[...] = jnp.full(m_sc.shape, -jnp.inf, F32)
    l_sc[...] = jnp.zeros(l_sc.shape, F32)
    acc_sc[...] = jnp.zeros(acc_sc.shape, F32)
    q = q_ref[0]
    q_lat, q_pe = q[:, :KV_LORA], q[:, KV_LORA:KV_LORA + ROPE_DIM]

    def scores(c, buf):
        g = b * n_chunks + c
        slot = lax.rem(g, DEC_SLOTS)

        @pl.when(g + ahead < n_total)
        def _():
            for cp in chunk_copies(g + ahead, True):
                cp.start()

        for cp in chunk_copies(g, False):
            cp.wait()
        ck = ck_buf[slot].astype(BF16)
        kp_t = kp_buf[slot].astype(BF16)
        kb_sc[buf] = ck
        s_sc[buf] = (_dot_nt(q_lat, ck) + _dot(q_pe, kp_t)) * SM_SCALE

    def update(buf):
        _softmax_update(s_sc[buf], kb_sc[buf], m_sc, l_sc, acc_sc)

    scores(0, 0)

    def chunk_pair(jj, carry):
        c = 2 * jj + 1
        scores(c, 1)
        update(0)
        scores(c + 1, 0)
        update(1)
        return carry

    lax.fori_loop(0, (n_chunks - 2) // 2, chunk_pair, 0)
    scores(n_chunks - 1, 1)
    update(0)
    update(1)

    kn = knew_ref[0]
    sn = _dot_nt(q, kn) * SM_SCALE
    tok = lax.broadcasted_iota(jnp.int32, sn.shape, 0) >> (N_HEADS.bit_length() - 1)
    key = lax.broadcasted_iota(jnp.int32, sn.shape, 1)
    sn = jnp.where((key <= tok) & (key < n_new), sn, NEG_BIG)
    _softmax_update(sn, kn[:, :KV_LORA], m_sc, l_sc, acc_sc)
    o_ref[0] = (acc_sc[...] / jnp.tile(l_sc[...], (1, KV_LORA // LANES))).astype(o_ref.dtype)


def _decode_attn(q, knew, cache_ckv, cache_kpe_t, page_table, layer, n_new):
    db, rows, _ = q.shape
    n_pages = page_table.shape[1]
    page = cache_ckv.shape[2]
    n_chunks = n_pages // DEC_PAGES
    assert n_pages % DEC_PAGES == 0 and n_chunks % 2 == 0 and db * n_chunks >= DEC_SLOTS
    keys = DEC_PAGES * page
    return pl.pallas_call(
        functools.partial(_decode_attn_kernel, layer=layer, n_new=n_new, n_chunks=n_chunks, page=page),
        out_shape=jax.ShapeDtypeStruct((db, rows, KV_LORA), BF16),
        grid_spec=pltpu.PrefetchScalarGridSpec(
            num_scalar_prefetch=1,
            grid=(db,),
            in_specs=[pl.BlockSpec((1, rows, QK_PAD), lambda b, pt: (b, 0, 0)),
                      pl.BlockSpec((1, DEC_NEW_PAD, QK_PAD), lambda b, pt: (b, 0, 0)),
                      pl.BlockSpec(memory_space=pl.ANY),
                      pl.BlockSpec(memory_space=pl.ANY)],
            out_specs=pl.BlockSpec((1, rows, KV_LORA), lambda b, pt: (b, 0, 0)),
            scratch_shapes=[pltpu.VMEM((DEC_SLOTS, keys, KV_LORA), F32),
                            pltpu.VMEM((DEC_SLOTS, ROPE_DIM, keys), F32),
                            pltpu.SemaphoreType.DMA((DEC_SLOTS,)),
                            pltpu.VMEM((2, rows, keys), F32),
                            pltpu.VMEM((2, keys, KV_LORA), BF16),
                            pltpu.VMEM((rows, LANES), F32), pltpu.VMEM((rows, LANES), F32),
                            pltpu.VMEM((rows, KV_LORA), F32)]),
        compiler_params=_cparams(1),
        name="decode_attn",
    )(page_table, q, knew, cache_ckv, cache_kpe_t)


UV_HEADS_PER_STEP = 4


def _uv_proj_kernel(x_ref, w_ref, o_ref):
    for h in range(UV_HEADS_PER_STEP):
        o_ref[:, h * V_DIM:(h + 1) * V_DIM] = _dot(
            x_ref[:, h * KV_LORA:(h + 1) * KV_LORA], w_ref[h]).astype(o_ref.dtype)


def _uv_proj(o_lat, wuv_t, tm):
    m = o_lat.shape[0]
    hu = UV_HEADS_PER_STEP
    return pl.pallas_call(
        _uv_proj_kernel,
        out_shape=jax.ShapeDtypeStruct((m, N_HEADS * V_DIM), BF16),
        grid=(m // tm, N_HEADS // hu),
        in_specs=[pl.BlockSpec((tm, hu * KV_LORA), lambda i, j: (i, j)),
                  pl.BlockSpec((hu, KV_LORA, V_DIM), lambda i, j: (j, 0, 0))],
        out_specs=pl.BlockSpec((tm, hu * V_DIM), lambda i, j: (i, j)),
        compiler_params=_cparams(2),
        name="uv_proj",
    )(o_lat, wuv_t)


CONV_TS = 256
CONV_HALO = 32
CONV_RC = 128
CONV_LN_ROWS = 64
SUBLANES = 8


def _ln_silu(y, g, b):
    yc = y - jnp.mean(y, axis=-1, keepdims=True)
    yn = yc * lax.rsqrt(jnp.mean(yc * yc, axis=-1, keepdims=True) + EPS) * g + b
    return yn * jax.nn.sigmoid(yn)


def _conv_prompt_kernel(halo_ref, x_ref, w_ref, bdw_ref, g_ref, b_ref, o_ref, xw_sc, y_sc, sh_sc):
    ts = CONV_TS
    i = pl.program_id(1)
    xw_sc[0:CONV_HALO, :] = jnp.where(i > 0, halo_ref[...], 0.0)
    xw_sc[CONV_HALO:CONV_HALO + ts, :] = x_ref[...]
    base = CONV_HALO - (CONV_W - 1)

    def col_chunk(c, carry):
        col = pl.multiple_of(c * LANES, LANES)
        for rc in range(ts // CONV_RC):
            row0 = rc * CONV_RC
            acc = jnp.zeros((CONV_RC, LANES), F32)
            for r in range(SUBLANES):
                taps = [(o // SUBLANES, o - base) for o in range(r, CONV_HALO + 1, SUBLANES)
                        if 0 <= o - base < CONV_W]
                j_lo, j_hi = taps[0][0], taps[-1][0]
                start = row0 + r + j_lo * SUBLANES
                n_rows = (j_hi - j_lo) * SUBLANES + CONV_RC
                sh_sc[r, 0:n_rows, :] = xw_sc[start:start + n_rows, pl.ds(col, LANES)]
                for j, k in taps:
                    off = (j - j_lo) * SUBLANES
                    acc = acc + w_ref[k:k + 1, pl.ds(col, LANES)] * sh_sc[r, off:off + CONV_RC, :]
            y_sc[row0:row0 + CONV_RC, pl.ds(col, LANES)] = acc + bdw_ref[:, pl.ds(col, LANES)]
        return carry

    lax.fori_loop(0, D_MODEL // LANES, col_chunk, 0)

    def ln_chunk(r, carry):
        rs = pl.ds(pl.multiple_of(r * CONV_LN_ROWS, CONV_LN_ROWS), CONV_LN_ROWS)
        o_ref[rs, :] = _ln_silu(y_sc[rs, :], g_ref[...], b_ref[...]).astype(o_ref.dtype)
        return carry

    lax.fori_loop(0, ts // CONV_LN_ROWS, ln_chunk, 0)


def _conv_prompt(u, w_dw, b_dw, g_ln, b_ln, batch, seq):
    ts = CONV_TS
    nt = seq // ts
    hb = ts // CONV_HALO
    fixed = lambda b, i: (0, 0)
    return pl.pallas_call(
        _conv_prompt_kernel,
        out_shape=jax.ShapeDtypeStruct((batch * seq, D_MODEL), BF16),
        grid=(batch, nt),
        in_specs=[pl.BlockSpec((CONV_HALO, D_MODEL),
                               lambda b, i: (jnp.maximum((b * nt + i) * hb - 1, 0), 0)),
                  pl.BlockSpec((ts, D_MODEL), lambda b, i: (b * nt + i, 0)),
                  pl.BlockSpec((CONV_W, D_MODEL), fixed),
                  pl.BlockSpec((1, D_MODEL), fixed),
                  pl.BlockSpec((1, D_MODEL), fixed),
                  pl.BlockSpec((1, D_MODEL), fixed)],
        out_specs=pl.BlockSpec((ts, D_MODEL), lambda b, i: (b * nt + i, 0)),
        scratch_shapes=[pltpu.VMEM((CONV_HALO + ts, D_MODEL), F32),
                        pltpu.VMEM((ts, D_MODEL), F32),
                        pltpu.VMEM((SUBLANES, CONV_HALO + CONV_RC, LANES), F32)],
        compiler_params=_cparams(2),
        name="conv_prompt",
    )(u, u, w_dw, b_dw.reshape(1, -1), g_ln.reshape(1, -1), b_ln.reshape(1, -1))


CONV_SB = 8


def _conv_sample_kernel(st_ref, u_ref, wt_ref, bdw_ref, g_ref, b_ref, o_ref, ns_ref, full_sc):
    n_prev = CONV_W - 1
    t = u_ref.shape[1]
    rows = full_sc.shape[0]
    full_sc[n_prev + t:rows, :] = jnp.zeros((rows - n_prev - t, D_MODEL), F32)

    def one(bi, carry):
        full_sc[0:n_prev, :] = st_ref[bi]
        full_sc[n_prev:n_prev + t, :] = u_ref[bi]
        full = full_sc[...]
        y = jnp.concatenate([jnp.sum(full * wt_ref[tt], axis=0, keepdims=True) for tt in range(t)],
                            axis=0)
        o_ref[bi] = _ln_silu(y + bdw_ref[...], g_ref[...], b_ref[...]).astype(o_ref.dtype)
        ns_ref[bi] = full_sc[t:t + n_prev, :]
        return carry

    lax.fori_loop(0, CONV_SB, one, 0)


def _conv_sample(state, u, w_dw, b_dw, g_ln, b_ln):
    db, n_prev, d = state.shape
    t = u.shape[1]
    sb = CONV_SB
    rows = -(-(n_prev + t) // SUBLANES) * SUBLANES
    w_shift = jnp.stack([jnp.pad(w_dw, ((tt, rows - CONV_W - tt), (0, 0))) for tt in range(t)])
    fixed = lambda i: (0, 0)
    blk = lambda i: (i, 0, 0)
    return pl.pallas_call(
        _conv_sample_kernel,
        out_shape=[jax.ShapeDtypeStruct((db, t, d), F32),
                   jax.ShapeDtypeStruct((db, n_prev, d), F32)],
        grid=(db // sb,),
        in_specs=[pl.BlockSpec((sb, n_prev, d), blk),
                  pl.BlockSpec((sb, t, d), blk),
                  pl.BlockSpec((t, rows, d), lambda i: (0, 0, 0)),
                  pl.BlockSpec((1, d), fixed),
                  pl.BlockSpec((1, d), fixed),
                  pl.BlockSpec((1, d), fixed)],
        out_specs=[pl.BlockSpec((sb, t, d), blk), pl.BlockSpec((sb, n_prev, d), blk)],
        scratch_shapes=[pltpu.VMEM((rows, d), F32)],
        compiler_params=_cparams(1),
        name="conv_sample",
    )(state, u, w_shift, b_dw.reshape(1, -1), g_ln.reshape(1, -1), b_ln.reshape(1, -1))


TM_PROMPT = 1024


def _residual_add(accs, ex):
    out = ex[0] + accs[0]
    for e in ex[1:]:
        out = out + e
    return [out]


_mm = _matmul


def _normed(h, pair, g):
    return pair if pair is not None else (_rmsnorm(h, g, BF16), None)


def _row_extras(stat):
    return [] if stat is None else [(stat, "row")]


def _scaled(accs, ex, has_stat, which):
    if not has_stat:
        return accs, ex
    r = _row_rinv(ex[0], accs[0].shape[1])
    return [a * r if k in which else a for k, a in enumerate(accs)], ex[1:]


def _linear_residual(xs, xp, hs, hp, w, layer, name, g_next, bias=None):
    ms = xs.shape[0]
    cols = [] if bias is None else [(bias.reshape(1, -1), "col")]
    (hs2,), ns, (wb,) = _mm([xs], [(0, w, layer, 0)], [(hs, "tile")] + cols, _residual_add, [F32],
                            n_cols=D_MODEL, tm=ms, tn=256, name=name, emit_w=True, norm_gain=g_next)
    (hp2,), np_, _ = _mm([xp], [(0, wb, 0, 0)], [(hp, "tile")] + cols, _residual_add, [F32],
                         n_cols=D_MODEL, tm=TM_PROMPT, tn=512, name=name, norm_gain=g_next)
    return hs2, ns, hp2, np_


def _ffn(hs, ns, hp, np_, g, wg, wu, wd, layer, g_next):
    ms = hs.shape[0]
    d_ff = wg.shape[2]
    (zs, ss), (zp, sp) = _normed(hs, ns, g), _normed(hp, np_, g)

    def gate_up(has_stat):
        def epilogue(accs, ex):
            (a, b), _ = _scaled(accs, ex, has_stat, (0, 1))
            return [jax.nn.silu(a) * b]
        return epilogue

    (act_s,), _, (wgb, wub) = _mm([zs], [(0, wg, layer, 0), (0, wu, layer, 0)], _row_extras(ss),
                                  gate_up(ss is not None), [BF16],
                                  n_cols=d_ff, tm=ms, tn=256, name="ffn_gate_up", emit_w=True)
    (act_p,), _, _ = _mm([zp], [(0, wgb, 0, 0), (0, wub, 0, 0)], _row_extras(sp),
                         gate_up(sp is not None), [BF16],
                         n_cols=d_ff, tm=TM_PROMPT, tn=256, name="ffn_gate_up")
    (hs2,), ns2, (wdb,) = _mm([act_s], [(0, wd, layer, 0)], [(hs, "tile")], _residual_add, [F32],
                              n_cols=D_MODEL, tm=ms, tn=128, name="ffn_down", emit_w=True,
                              norm_gain=g_next)
    (hp2,), np2, _ = _mm([act_p], [(0, wdb, 0, 0)], [(hp, "tile")], _residual_add, [F32],
                         n_cols=D_MODEL, tm=512, tn=512, name="ffn_down", norm_gain=g_next)
    return hs2, ns2, hp2, np2


def _ple(hs, ns, hp, np_, ps, pp, g, w_gate, w_up, layer, g_next):
    ms = hs.shape[0]
    (zs, ss), (zp, sp) = _normed(hs, ns, g), _normed(hp, np_, g)

    def gated(has_stat):
        def epilogue(accs, ex):
            (a, b), ex = _scaled(accs, ex, has_stat, (0,))
            return [ex[0] + jax.nn.sigmoid(a) * b]
        return epilogue

    (hs2,), ns2, (wgb, wub) = _mm([zs, ps], [(0, w_gate, layer, 0), (1, w_up, layer, 0)],
                                  _row_extras(ss) + [(hs, "tile")], gated(ss is not None), [F32],
                                  n_cols=D_MODEL, tm=ms, tn=256, name="ple", emit_w=True,
                                  norm_gain=g_next)
    (hp2,), np2, _ = _mm([zp, pp], [(0, wgb, 0, 0), (1, wub, 0, 0)],
                         _row_extras(sp) + [(hp, "tile")], gated(sp is not None), [F32],
                         n_cols=D_MODEL, tm=TM_PROMPT, tn=512, name="ple", norm_gain=g_next)
    return hs2, ns2, hp2, np2


def _rope_tables(pos):
    half = ROPE_DIM // 2
    inv_freq = ROPE_THETA ** (-jnp.arange(half, dtype=F32) / half)
    ang = pos.astype(F32)[:, None] * inv_freq
    cos, sin = jnp.cos(ang), jnp.sin(ang)
    pad = jnp.zeros((pos.shape[0], ROPE_PAD - ROPE_DIM), F32)
    return (jnp.concatenate([cos, cos, pad], axis=1),
            jnp.concatenate([-sin, sin, pad], axis=1))


def _swap_halves(w):
    half = ROPE_DIM // 2
    return jnp.concatenate([w[..., half:], w[..., :half]], axis=-1)


def _pad_rope(w):
    return jnp.concatenate([w, jnp.zeros(w.shape[:-1] + (ROPE_PAD - ROPE_DIM,), w.dtype)], axis=-1)


def kernel(x_prompt, x_sample, cache_ckv, cache_kpe, state_conv, page_table, p_prompt, p_sample,
           g_attn_norm, w_dq, g_q, w_uq, w_dkv, g_kv, w_uk, w_uv, w_o,
           g_conv_norm, w_pw1, b_pw1, w_dw, b_dw, g_conv_ln, b_conv_ln, w_pw2, b_pw2,
           g_ffn_norm, w_ffn_gate, w_ffn_up, w_ffn_down,
           g_ple_norm, w_ple_gate, w_ple_up, g_final):
    B, S, _ = x_prompt.shape
    DB, T, _ = x_sample.shape
    depth = g_ffn_norm.shape[0]
    past = page_table.shape[1] * cache_ckv.shape[2]
    MP, MS = B * S, DB * T

    cos_p, sin_p = _rope_tables(jnp.tile(jnp.arange(S, dtype=jnp.int32), B))
    cos_s, sin_s = _rope_tables(jnp.tile(past + jnp.arange(T, dtype=jnp.int32), DB))

    hp = x_prompt.reshape(MP, D_MODEL)
    hs = x_sample.reshape(MS, D_MODEL)
    ckv_p, kpe_p, conv_p, ckv_s, kpe_s, conv_s = [], [], [], [], [], []
    ns = np_ = None

    for i in range(depth):
        j = i // 2
        if i % 2 == 0:
            wdq = w_dq[j].astype(BF16)
            w_kpe = w_dkv[j][:, KV_LORA:]
            wkv = jnp.concatenate([w_dkv[j][:, :KV_LORA], _pad_rope(w_kpe),
                                   _pad_rope(_swap_halves(w_kpe))], axis=1).astype(BF16)
            wq3 = w_uq[j].reshape(Q_LORA, N_HEADS, QK_DIM)
            wn = wq3[:, :, :NOPE_DIM].reshape(Q_LORA, N_HEADS * NOPE_DIM).astype(BF16)
            wp = _pad_rope(wq3[:, :, NOPE_DIM:]).reshape(Q_LORA, N_HEADS * ROPE_PAD).astype(BF16)
            wps = _pad_rope(_swap_halves(wq3[:, :, NOPE_DIM:])).reshape(
                Q_LORA, N_HEADS * ROPE_PAD).astype(BF16)
            wuk_t = jnp.transpose(w_uk[j], (1, 2, 0)).astype(BF16)
            wuv_t = jnp.transpose(w_uv[j], (1, 0, 2)).astype(BF16)

            z = _rmsnorm(hs, g_attn_norm[j], BF16)
            cq, ck, kp, kk = _kv_proj(z, wdq, wkv, g_q[j], g_kv[j], cos_s, sin_s)
            q = _q_proj(cq, wn, wp, wps, wuk_t, cos_s, sin_s, tm=MS)
            knew = jnp.pad(kk.reshape(DB, T, QK_PAD), ((0, 0), (0, DEC_NEW_PAD - T), (0, 0)))
            o_lat = _decode_attn(q.reshape(DB, T * N_HEADS, QK_PAD), knew,
                                 cache_ckv, jnp.swapaxes(cache_kpe, 2, 3), page_table, j, T)
            o_s = _uv_proj(o_lat.reshape(MS, N_HEADS * KV_LORA), wuv_t, tm=MS)
            ckv_s.append(ck.reshape(DB, T, KV_LORA))
            kpe_s.append(kp.reshape(DB, T, ROPE_DIM))

            z = _rmsnorm(hp, g_attn_norm[j], BF16)
            cq, ck, kp, kk = _kv_proj(z, wdq, wkv, g_q[j], g_kv[j], cos_p, sin_p)
            q = _q_proj(cq, wn, wp, wps, wuk_t, cos_p, sin_p, tm=TM_PROMPT)
            o_p = _prompt_attn(q, kk, wuv_t, B, S)
            ckv_p.append(ck.reshape(B, S, KV_LORA))
            kpe_p.append(kp.reshape(B, S, ROPE_DIM))

            hs, ns, hp, np_ = _linear_residual(o_s, o_p, hs, hp, w_o, j, "attn_out", g_ffn_norm[i])
        else:
            b1 = b_pw1[j].reshape(1, -1)
            biases = [(b1[:, :D_MODEL], "col"), (b1[:, D_MODEL:], "col")]
            (zs, ss), (zp, sp) = _normed(hs, ns, g_conv_norm[j]), _normed(hp, np_, g_conv_norm[j])

            def glu(has_stat):
                def epilogue(accs, ex):
                    (a, b), ex = _scaled(accs, ex, has_stat, (0, 1))
                    return [(a + ex[0]) * jax.nn.sigmoid(b + ex[1])]
                return epilogue

            (u_s,), _, (w1a, w1b) = _mm([zs], [(0, w_pw1, j, 0), (0, w_pw1, j, D_MODEL // 256)],
                                        _row_extras(ss) + biases, glu(ss is not None), [F32],
                                        n_cols=D_MODEL, tm=MS, tn=256, name="pw1_glu", emit_w=True)
            (u_p,), _, _ = _mm([zp], [(0, w1a, 0, 0), (0, w1b, 0, 0)], _row_extras(sp) + biases,
                               glu(sp is not None), [F32],
                               n_cols=D_MODEL, tm=TM_PROMPT, tn=512, name="pw1_glu")
            c_s, new_state = _conv_sample(state_conv[j], u_s.reshape(DB, T, D_MODEL),
                                          w_dw[j], b_dw[j], g_conv_ln[j], b_conv_ln[j])
            c_p = _conv_prompt(u_p, w_dw[j], b_dw[j], g_conv_ln[j], b_conv_ln[j], B, S)
            conv_s.append(new_state)
            conv_p.append(u_p.reshape(B, S, D_MODEL)[:, S - (CONV_W - 1):])
            hs, ns, hp, np_ = _linear_residual(c_s.reshape(MS, D_MODEL), c_p, hs, hp, w_pw2, j, "pw2",
                                               g_ffn_norm[i], bias=b_pw2[j])

        hs, ns, hp, np_ = _ffn(hs, ns, hp, np_, g_ffn_norm[i], w_ffn_gate, w_ffn_up, w_ffn_down, i,
                               g_ple_norm[i])
        next_is_conv = i + 1 < depth and (i + 1) % 2 == 1
        hs, ns, hp, np_ = _ple(hs, ns, hp, np_, p_sample[i].reshape(MS, -1), p_prompt[i].reshape(MP, -1),
                               g_ple_norm[i], w_ple_gate, w_ple_up, i,
                               g_conv_norm[(i + 1) // 2] if next_is_conv else None)

    y_prompt = _rmsnorm(hp, g_final, F32).reshape(B, S, D_MODEL)
    y_sample = _rmsnorm(hs, g_final, F32).reshape(DB, T, D_MODEL)
    return (y_prompt, y_sample,
            jnp.stack(ckv_p), jnp.stack(kpe_p), jnp.stack(conv_p),
            jnp.stack(ckv_s), jnp.stack(kpe_s), jnp.stack(conv_s))
```

```python
import functools
import math

import jax
import jax.numpy as jnp
from jax import lax
from jax.experimental import pallas as pl
from jax.experimental.pallas import tpu as pltpu

D_MODEL = 4096
N_HEADS = 32
Q_LORA = 1024
KV_LORA = 512
NOPE_DIM = 128
ROPE_DIM = 64
V_DIM = 128
QK_DIM = NOPE_DIM + ROPE_DIM
SM_SCALE = 1.0 / math.sqrt(QK_DIM)
ROPE_THETA = 10000.0
CONV_W = 31
EPS = 1e-6

LANES = 128
ROPE_PAD = LANES
QK_PAD = KV_LORA + ROPE_PAD
NEG_BIG = -1e30
VMEM_LIMIT = 56 * 1024 * 1024

F32 = jnp.float32
BF16 = jnp.bfloat16


def _cparams(n_axes):
    return pltpu.CompilerParams(dimension_semantics=("arbitrary",) * n_axes,
                                vmem_limit_bytes=VMEM_LIMIT)


def _dot(a, b):
    return jnp.dot(a, b, preferred_element_type=F32)


def _dot_nt(a, b):
    return lax.dot_general(a, b, (((1,), (1,)), ((), ())), preferred_element_type=F32)


def _rmsnorm_kernel(x_ref, g_ref, o_ref):
    x = x_ref[...]
    y = x * lax.rsqrt(jnp.mean(x * x, axis=-1, keepdims=True) + EPS)
    o_ref[...] = (y * g_ref[...]).astype(o_ref.dtype)


def _rmsnorm(x, g, out_dtype, tm=256):
    m, d = x.shape
    tm = min(tm, m)
    return pl.pallas_call(
        _rmsnorm_kernel,
        out_shape=jax.ShapeDtypeStruct((m, d), out_dtype),
        grid=(m // tm,),
        in_specs=[pl.BlockSpec((tm, d), lambda i: (i, 0)),
                  pl.BlockSpec((1, d), lambda i: (0, 0))],
        out_specs=pl.BlockSpec((tm, d), lambda i: (i, 0)),
        compiler_params=_cparams(1),
        name="rmsnorm",
    )(x, g.reshape(1, d))


def _matmul(xs, wspecs, extras, epilogue, out_dtypes, *, n_cols, tm, tn, name,
            emit_w=False, norm_gain=None):
    m = xs[0].shape[0]
    tm = min(tm, m)
    nx, nw, ne, no = len(xs), len(wspecs), len(extras), len(out_dtypes)
    has_norm = norm_gain is not None
    assert not emit_w or m == tm
    assert not has_norm or out_dtypes[0] == F32
    in_specs = [pl.BlockSpec((tm, x.shape[1]), lambda i, j: (i, 0)) for x in xs]
    for _, w, layer, off in wspecs:
        in_specs.append(pl.BlockSpec((None, w.shape[1], tn),
                                     lambda i, j, layer=layer, off=off: (layer, 0, j + off)))
    for _, kind in extras:
        if kind == "tile":
            in_specs.append(pl.BlockSpec((tm, tn), lambda i, j: (i, j)))
        elif kind == "row":
            in_specs.append(pl.BlockSpec((tm, LANES), lambda i, j: (i, 0)))
        else:
            in_specs.append(pl.BlockSpec((1, tn), lambda i, j: (0, j)))
    inputs = [*xs, *[w for _, w, _, _ in wspecs], *[a for a, _ in extras]]
    if has_norm:
        in_specs.append(pl.BlockSpec((1, tn), lambda i, j: (0, j)))
        inputs.append(norm_gain.reshape(1, n_cols))
    x_index = [xi for xi, _, _, _ in wspecs]
    n_in = nx + nw + ne + has_norm

    def kernel(*refs):
        x_refs, w_refs = refs[:nx], refs[nx:nx + nw]
        e_refs = refs[nx + nw:nx + nw + ne]
        o_refs = refs[n_in:n_in + no]
        rest = refs[n_in + no:]
        xv = [r[...].astype(BF16) for r in x_refs]
        wv = [r[...].astype(BF16) for r in w_refs]
        accs = [_dot(xv[xi], w) for xi, w in zip(x_index, wv)]
        outs = epilogue(accs, [e[...] for e in e_refs])
        for o_ref, o in zip(o_refs, outs):
            o_ref[...] = o.astype(o_ref.dtype)
        if has_norm:
            hb_ref, ssq_ref = rest[:2]
            rest = rest[2:]
            h = outs[0]
            hb_ref[...] = (h * refs[n_in - 1][...]).astype(BF16)
            part = jnp.broadcast_to(jnp.sum(h * h, axis=1, keepdims=True), ssq_ref.shape)

            @pl.when(pl.program_id(1) == 0)
            def _():
                ssq_ref[...] = part

            @pl.when(pl.program_id(1) > 0)
            def _():
                ssq_ref[...] += part

            @pl.when(pl.program_id(1) == pl.num_programs(1) - 1)
            def _():
                ssq_ref[...] = lax.rsqrt(ssq_ref[...] * (1.0 / n_cols) + EPS)
        for wo_ref, w in zip(rest, wv):
            wo_ref[...] = w

    out_shape = [jax.ShapeDtypeStruct((m, n_cols), dt) for dt in out_dtypes]
    out_specs = [pl.BlockSpec((tm, tn), lambda i, j: (i, j)) for _ in range(no)]
    if has_norm:
        out_shape += [jax.ShapeDtypeStruct((m, n_cols), BF16), jax.ShapeDtypeStruct((m, LANES), F32)]
        out_specs += [pl.BlockSpec((tm, tn), lambda i, j: (i, j)),
                      pl.BlockSpec((tm, LANES), lambda i, j: (i, 0))]
    if emit_w:
        for _, w, _, _ in wspecs:
            out_shape.append(jax.ShapeDtypeStruct((1, w.shape[1], n_cols), BF16))
            out_specs.append(pl.BlockSpec((None, w.shape[1], tn), lambda i, j: (0, 0, j)))
    outs = pl.pallas_call(
        kernel,
        out_shape=out_shape,
        grid=(m // tm, n_cols // tn),
        in_specs=in_specs,
        out_specs=out_specs,
        compiler_params=_cparams(2),
        name=name,
    )(*inputs)
    n_norm = 2 if has_norm else 0
    return list(outs[:no]), (tuple(outs[no:no + n_norm]) if has_norm else None), list(outs[no + n_norm:])


def _row_rinv(rinv, width):
    return jnp.tile(rinv, (1, width // LANES))


def _kv_proj_kernel(z_ref, wdq_ref, wkv_ref, gq_ref, gkv_ref, cos_ref, sin_ref,
                    cq_ref, ckv_ref, kpe_ref, k_ref):
    z = z_ref[...]
    a = _dot(z, wdq_ref[...])
    cq = a * lax.rsqrt(jnp.mean(a * a, axis=-1, keepdims=True) + EPS) * gq_ref[...]
    cq_ref[...] = cq.astype(cq_ref.dtype)
    kv = _dot(z, wkv_ref[...])
    c = kv[:, :KV_LORA]
    ckv = c * lax.rsqrt(jnp.mean(c * c, axis=-1, keepdims=True) + EPS) * gkv_ref[...]
    kpe = (kv[:, KV_LORA:KV_LORA + ROPE_PAD] * cos_ref[...]
           + kv[:, KV_LORA + ROPE_PAD:] * sin_ref[...])
    ckv_ref[...] = ckv
    kpe_ref[...] = kpe[:, :ROPE_DIM]
    k_ref[:, :KV_LORA] = ckv.astype(k_ref.dtype)
    k_ref[:, KV_LORA:] = kpe.astype(k_ref.dtype)


def _kv_proj(z, wdq, wkv, gq, gkv, cos, sin, tm=512):
    m = z.shape[0]
    tm = min(tm, m)
    row = lambda i: (i, 0)
    fixed = lambda i: (0, 0)
    return pl.pallas_call(
        _kv_proj_kernel,
        out_shape=[jax.ShapeDtypeStruct((m, Q_LORA), BF16),
                   jax.ShapeDtypeStruct((m, KV_LORA), F32),
                   jax.ShapeDtypeStruct((m, ROPE_DIM), F32),
                   jax.ShapeDtypeStruct((m, QK_PAD), BF16)],
        grid=(m // tm,),
        in_specs=[pl.BlockSpec((tm, D_MODEL), row),
                  pl.BlockSpec(wdq.shape, fixed),
                  pl.BlockSpec(wkv.shape, fixed),
                  pl.BlockSpec((1, Q_LORA), fixed),
                  pl.BlockSpec((1, KV_LORA), fixed),
                  pl.BlockSpec((tm, ROPE_PAD), row),
                  pl.BlockSpec((tm, ROPE_PAD), row)],
        out_specs=[pl.BlockSpec((tm, Q_LORA), row),
                   pl.BlockSpec((tm, KV_LORA), row),
                   pl.BlockSpec((tm, ROPE_DIM), row),
                   pl.BlockSpec((tm, QK_PAD), row)],
        compiler_params=_cparams(1),
        name="kv_proj",
    )(z, wdq, wkv, gq.reshape(1, -1), gkv.reshape(1, -1), cos, sin)


Q_HEADS_PER_STEP = 4


def _q_proj_kernel(cq_ref, wn_ref, wp_ref, wuk_ref, tab_ref, q_ref):
    cq = cq_ref[...]
    qn = _dot(cq, wn_ref[...]).astype(BF16)
    qp = _dot(cq, wp_ref[...])
    tab = tab_ref[...]
    for h in range(Q_HEADS_PER_STEP):
        ql = _dot(qn[:, h * NOPE_DIM:(h + 1) * NOPE_DIM], wuk_ref[h])
        y = qp[:, h * ROPE_PAD:(h + 1) * ROPE_PAD] * tab
        qr = y + pltpu.roll(y, ROPE_DIM, axis=1)
        q_ref[:, h * QK_PAD:h * QK_PAD + KV_LORA] = ql.astype(q_ref.dtype)
        q_ref[:, h * QK_PAD + KV_LORA:(h + 1) * QK_PAD] = qr.astype(q_ref.dtype)


def _q_proj(cq, wn, wp, wuk_t, tab, tm):
    m = cq.shape[0]
    tm = min(tm, m)
    hq = Q_HEADS_PER_STEP
    return pl.pallas_call(
        _q_proj_kernel,
        out_shape=jax.ShapeDtypeStruct((m, N_HEADS * QK_PAD), BF16),
        grid=(m // tm, N_HEADS // hq),
        in_specs=[pl.BlockSpec((tm, Q_LORA), lambda i, j: (i, 0)),
                  pl.BlockSpec((Q_LORA, hq * NOPE_DIM), lambda i, j: (0, j)),
                  pl.BlockSpec((Q_LORA, hq * ROPE_PAD), lambda i, j: (0, j)),
                  pl.BlockSpec((hq, NOPE_DIM, KV_LORA), lambda i, j: (j, 0, 0)),
                  pl.BlockSpec((tm, ROPE_PAD), lambda i, j: (i, 0))],
        out_specs=pl.BlockSpec((tm, hq * QK_PAD), lambda i, j: (i, j)),
        compiler_params=_cparams(2),
        name="q_proj",
    )(cq, wn, wp, wuk_t, tab)


def _softmax_update(s, v, m_sc, l_sc, acc_sc):
    n = s.shape[1]
    m_prev = m_sc[...]
    m_new = jnp.maximum(m_prev, jnp.max(s, axis=1, keepdims=True))
    alpha = jnp.exp(m_prev - m_new)
    m_wide = jnp.tile(m_new, (1, n // LANES)) if n % LANES == 0 else m_new[:, :n]
    p = jnp.exp(s - m_wide)
    l_sc[...] = alpha * l_sc[...] + jnp.sum(p, axis=1, keepdims=True)
    acc_sc[...] = jnp.tile(alpha, (1, KV_LORA // LANES)) * acc_sc[...] + _dot(p.astype(BF16), v)
    m_sc[...] = m_new


PROMPT_TQ = 512
PROMPT_HG = 4


def _prompt_attn_kernel(q_ref, k_ref, wuv_ref, o_ref, m_sc, l_sc, acc_sc):
    tq, hg = PROMPT_TQ, PROMPT_HG
    i = pl.program_id(1)
    q = jnp.concatenate([q_ref[:, h * QK_PAD:(h + 1) * QK_PAD] for h in range(hg)], axis=0)
    m_sc[...] = jnp.full(m_sc.shape, -jnp.inf, F32)
    l_sc[...] = jnp.zeros(l_sc.shape, F32)
    acc_sc[...] = jnp.zeros(acc_sc.shape, F32)

    def full_tile(kt, carry):
        k = k_ref[pl.ds(pl.multiple_of(kt * tq, tq), tq), :]
        s = _dot_nt(q, k) * SM_SCALE
        _softmax_update(s, k[:, :KV_LORA], m_sc, l_sc, acc_sc)
        return carry

    lax.fori_loop(0, i, full_tile, 0)

    k = k_ref[pl.ds(pl.multiple_of(i * tq, tq), tq), :]
    s = _dot_nt(q, k) * SM_SCALE
    tok = lax.broadcasted_iota(jnp.int32, s.shape, 0) & (tq - 1)
    key = lax.broadcasted_iota(jnp.int32, s.shape, 1)
    s = jnp.where(key <= tok, s, NEG_BIG)
    _softmax_update(s, k[:, :KV_LORA], m_sc, l_sc, acc_sc)

    o_lat = (acc_sc[...] / jnp.tile(l_sc[...], (1, KV_LORA // LANES))).astype(BF16)
    for h in range(hg):
        o_ref[:, h * V_DIM:(h + 1) * V_DIM] = _dot(
            o_lat[h * tq:(h + 1) * tq, :], wuv_ref[h]).astype(o_ref.dtype)


def _prompt_attn(q, k, wuv_t, batch, seq):
    tq, hg = PROMPT_TQ, PROMPT_HG
    nq = seq // tq
    rows = tq * hg
    return pl.pallas_call(
        _prompt_attn_kernel,
        out_shape=jax.ShapeDtypeStruct((batch * seq, N_HEADS * V_DIM), BF16),
        grid=(batch, nq, N_HEADS // hg),
        in_specs=[pl.BlockSpec((tq, hg * QK_PAD), lambda b, i, g: (b * nq + i, g)),
                  pl.BlockSpec((seq, QK_PAD), lambda b, i, g: (b, 0)),
                  pl.BlockSpec((hg, KV_LORA, V_DIM), lambda b, i, g: (g, 0, 0))],
        out_specs=pl.BlockSpec((tq, hg * V_DIM), lambda b, i, g: (b * nq + i, g)),
        scratch_shapes=[pltpu.VMEM((rows, LANES), F32), pltpu.VMEM((rows, LANES), F32),
                        pltpu.VMEM((rows, KV_LORA), F32)],
        compiler_params=_cparams(3),
        name="prompt_attn",
    )(q, k, wuv_t)


DEC_PAGES = 16
DEC_SLOTS = 3
DEC_NEW_PAD = 16


def _decode_attn_kernel(pt_ref, q_ref, knew_ref, ck_hbm, kp_hbm, o_ref,
                        ck_buf, kp_buf, sem, s_sc, kb_sc, m_sc, l_sc, acc_sc,
                        *, layer, n_new, n_chunks, page):
    b = pl.program_id(0)
    n_total = pl.num_programs(0) * n_chunks
    ahead = DEC_SLOTS - 1

    def chunk_copies(g, lookup):
        slot = lax.rem(g, DEC_SLOTS)
        if lookup:
            bb = lax.div(g, n_chunks)
            c = g - bb * n_chunks
        copies = []
        for p in range(DEC_PAGES):
            pg = pt_ref[bb, c * DEC_PAGES + p] if lookup else 0
            rows = pl.ds(p * page, page)
            copies.append(pltpu.make_async_copy(
                ck_hbm.at[layer, pg], ck_buf.at[slot, rows, :], sem.at[slot]))
            copies.append(pltpu.make_async_copy(
                kp_hbm.at[layer, pg], kp_buf.at[slot, :, rows], sem.at[slot]))
        return copies

    @pl.when(b == 0)
    def _():
        for g in range(ahead):
            for cp in chunk_copies(jnp.int32(g), True):
                cp.start()

    m_sc[...] = jnp.full(m_sc.shape, -jnp.inf, F32)
    l_sc[...] = jnp.zeros(l_sc.shape, F32)
    acc_sc[...] = jnp.zeros(acc_sc.shape, F32)
    q = q_ref[0]
    q_lat, q_pe = q[:, :KV_LORA], q[:, KV_LORA:KV_LORA + ROPE_DIM]

    def scores(c, buf):
        g = b * n_chunks + c
        slot = lax.rem(g, DEC_SLOTS)

        @pl.when(g + ahead < n_total)
        def _():
            for cp in chunk_copies(g + ahead, True):
                cp.start()

        for cp in chunk_copies(g, False):
            cp.wait()
        ck = ck_buf[slot].astype(BF16)
        kp_t = kp_buf[slot].astype(BF16)
        kb_sc[buf] = ck
        s_sc[buf] = (_dot_nt(q_lat, ck) + _dot(q_pe, kp_t)) * SM_SCALE

    def update(buf):
        _softmax_update(s_sc[buf], kb_sc[buf], m_sc, l_sc, acc_sc)

    scores(0, 0)

    def chunk_pair(jj, carry):
        c = 2 * jj + 1
        scores(c, 1)
        update(0)
        scores(c + 1, 0)
        update(1)
        return carry

    lax.fori_loop(0, (n_chunks - 2) // 2, chunk_pair, 0)
    scores(n_chunks - 1, 1)
    update(0)
    update(1)

    kn = knew_ref[0]
    sn = _dot_nt(q, kn) * SM_SCALE
    tok = lax.broadcasted_iota(jnp.int32, sn.shape, 0) >> (N_HEADS.bit_length() - 1)
    key = lax.broadcasted_iota(jnp.int32, sn.shape, 1)
    sn = jnp.where((key <= tok) & (key < n_new), sn, NEG_BIG)
    _softmax_update(sn, kn[:, :KV_LORA], m_sc, l_sc, acc_sc)
    o_ref[0] = (acc_sc[...] / jnp.tile(l_sc[...], (1, KV_LORA // LANES))).astype(o_ref.dtype)


def _decode_attn(q, knew, cache_ckv, cache_kpe_t, page_table, layer, n_new):
    db, rows, _ = q.shape
    n_pages = page_table.shape[1]
    page = cache_ckv.shape[2]
    n_chunks = n_pages // DEC_PAGES
    assert n_pages % DEC_PAGES == 0 and n_chunks % 2 == 0 and db * n_chunks >= DEC_SLOTS
    keys = DEC_PAGES * page
    return pl.pallas_call(
        functools.partial(_decode_attn_kernel, layer=layer, n_new=n_new, n_chunks=n_chunks, page=page),
        out_shape=jax.ShapeDtypeStruct((db, rows, KV_LORA), BF16),
        grid_spec=pltpu.PrefetchScalarGridSpec(
            num_scalar_prefetch=1,
            grid=(db,),
            in_specs=[pl.BlockSpec((1, rows, QK_PAD), lambda b, pt: (b, 0, 0)),
                      pl.BlockSpec((1, DEC_NEW_PAD, QK_PAD), lambda b, pt: (b, 0, 0)),
                      pl.BlockSpec(memory_space=pl.ANY),
                      pl.BlockSpec(memory_space=pl.ANY)],
            out_specs=pl.BlockSpec((1, rows, KV_LORA), lambda b, pt: (b, 0, 0)),
            scratch_shapes=[pltpu.VMEM((DEC_SLOTS, keys, KV_LORA), F32),
                            pltpu.VMEM((DEC_SLOTS, ROPE_DIM, keys), F32),
                            pltpu.SemaphoreType.DMA((DEC_SLOTS,)),
                            pltpu.VMEM((2, rows, keys), F32),
                            pltpu.VMEM((2, keys, KV_LORA), BF16),
                            pltpu.VMEM((rows, LANES), F32), pltpu.VMEM((rows, LANES), F32),
                            pltpu.VMEM((rows, KV_LORA), F32)]),
        compiler_params=_cparams(1),
        name="decode_attn",
    )(page_table, q, knew, cache_ckv, cache_kpe_t)


UV_HEADS_PER_STEP = 4


def _uv_proj_kernel(x_ref, w_ref, o_ref):
    for h in range(UV_HEADS_PER_STEP):
        o_ref[:, h * V_DIM:(h + 1) * V_DIM] = _dot(
            x_ref[:, h * KV_LORA:(h + 1) * KV_LORA], w_ref[h]).astype(o_ref.dtype)


def _uv_proj(o_lat, wuv_t, tm):
    m = o_lat.shape[0]
    hu = UV_HEADS_PER_STEP
    return pl.pallas_call(
        _uv_proj_kernel,
        out_shape=jax.ShapeDtypeStruct((m, N_HEADS * V_DIM), BF16),
        grid=(m // tm, N_HEADS // hu),
        in_specs=[pl.BlockSpec((tm, hu * KV_LORA), lambda i, j: (i, j)),
                  pl.BlockSpec((hu, KV_LORA, V_DIM), lambda i, j: (j, 0, 0))],
        out_specs=pl.BlockSpec((tm, hu * V_DIM), lambda i, j: (i, j)),
        compiler_params=_cparams(2),
        name="uv_proj",
    )(o_lat, wuv_t)


CONV_TS = 256
CONV_HALO = 32
CONV_RC = 128
CONV_LN_ROWS = 64
SUBLANES = 8


def _ln_silu(y, g, b):
    yc = y - jnp.mean(y, axis=-1, keepdims=True)
    yn = yc * lax.rsqrt(jnp.mean(yc * yc, axis=-1, keepdims=True) + EPS) * g + b
    return yn * jax.nn.sigmoid(yn)


def _conv_prompt_kernel(halo_ref, x_ref, w_ref, bdw_ref, g_ref, b_ref, o_ref, xw_sc, y_sc, sh_sc):
    ts = CONV_TS
    i = pl.program_id(1)
    xw_sc[0:CONV_HALO, :] = jnp.where(i > 0, halo_ref[...], 0.0)
    xw_sc[CONV_HALO:CONV_HALO + ts, :] = x_ref[...]
    base = CONV_HALO - (CONV_W - 1)

    def col_chunk(c, carry):
        col = pl.multiple_of(c * LANES, LANES)
        for rc in range(ts // CONV_RC):
            row0 = rc * CONV_RC
            acc = jnp.zeros((CONV_RC, LANES), F32)
            for r in range(SUBLANES):
                taps = [(o // SUBLANES, o - base) for o in range(r, CONV_HALO + 1, SUBLANES)
                        if 0 <= o - base < CONV_W]
                j_lo, j_hi = taps[0][0], taps[-1][0]
                start = row0 + r + j_lo * SUBLANES
                n_rows = (j_hi - j_lo) * SUBLANES + CONV_RC
                sh_sc[r, 0:n_rows, :] = xw_sc[start:start + n_rows, pl.ds(col, LANES)]
                for j, k in taps:
                    off = (j - j_lo) * SUBLANES
                    acc = acc + w_ref[k:k + 1, pl.ds(col, LANES)] * sh_sc[r, off:off + CONV_RC, :]
            y_sc[row0:row0 + CONV_RC, pl.ds(col, LANES)] = acc + bdw_ref[:, pl.ds(col, LANES)]
        return carry

    lax.fori_loop(0, D_MODEL // LANES, col_chunk, 0)

    def ln_chunk(r, carry):
        rs = pl.ds(pl.multiple_of(r * CONV_LN_ROWS, CONV_LN_ROWS), CONV_LN_ROWS)
        o_ref[rs, :] = _ln_silu(y_sc[rs, :], g_ref[...], b_ref[...]).astype(o_ref.dtype)
        return carry

    lax.fori_loop(0, ts // CONV_LN_ROWS, ln_chunk, 0)


def _conv_prompt(u, w_dw, b_dw, g_ln, b_ln, batch, seq):
    ts = CONV_TS
    nt = seq // ts
    hb = ts // CONV_HALO
    fixed = lambda b, i: (0, 0)
    return pl.pallas_call(
        _conv_prompt_kernel,
        out_shape=jax.ShapeDtypeStruct((batch * seq, D_MODEL), BF16),
        grid=(batch, nt),
        in_specs=[pl.BlockSpec((CONV_HALO, D_MODEL),
                               lambda b, i: (jnp.maximum((b * nt + i) * hb - 1, 0), 0)),
                  pl.BlockSpec((ts, D_MODEL), lambda b, i: (b * nt + i, 0)),
                  pl.BlockSpec((CONV_W, D_MODEL), fixed),
                  pl.BlockSpec((1, D_MODEL), fixed),
                  pl.BlockSpec((1, D_MODEL), fixed),
                  pl.BlockSpec((1, D_MODEL), fixed)],
        out_specs=pl.BlockSpec((ts, D_MODEL), lambda b, i: (b * nt + i, 0)),
        scratch_shapes=[pltpu.VMEM((CONV_HALO + ts, D_MODEL), F32),
                        pltpu.VMEM((ts, D_MODEL), F32),
                        pltpu.VMEM((SUBLANES, CONV_HALO + CONV_RC, LANES), F32)],
        compiler_params=_cparams(2),
        name="conv_prompt",
    )(u, u, w_dw, b_dw.reshape(1, -1), g_ln.reshape(1, -1), b_ln.reshape(1, -1))


CONV_SB = 8


def _conv_sample_kernel(st_ref, u_ref, wt_ref, bdw_ref, g_ref, b_ref, o_ref, ns_ref, full_sc):
    n_prev = CONV_W - 1
    t = u_ref.shape[1]
    rows = full_sc.shape[0]
    full_sc[n_prev + t:rows, :] = jnp.zeros((rows - n_prev - t, D_MODEL), F32)

    def one(bi, carry):
        full_sc[0:n_prev, :] = st_ref[bi]
        full_sc[n_prev:n_prev + t, :] = u_ref[bi]
        full = full_sc[...]
        y = jnp.concatenate([jnp.sum(full * wt_ref[tt], axis=0, keepdims=True) for tt in range(t)],
                            axis=0)
        o_ref[bi] = _ln_silu(y + bdw_ref[...], g_ref[...], b_ref[...]).astype(o_ref.dtype)
        ns_ref[bi] = full_sc[t:t + n_prev, :]
        return carry

    lax.fori_loop(0, CONV_SB, one, 0)


def _conv_sample(state, u, w_dw, b_dw, g_ln, b_ln):
    db, n_prev, d = state.shape
    t = u.shape[1]
    sb = CONV_SB
    rows = -(-(n_prev + t) // SUBLANES) * SUBLANES
    w_shift = jnp.stack([jnp.pad(w_dw, ((tt, rows - CONV_W - tt), (0, 0))) for tt in range(t)])
    fixed = lambda i: (0, 0)
    blk = lambda i: (i, 0, 0)
    return pl.pallas_call(
        _conv_sample_kernel,
        out_shape=[jax.ShapeDtypeStruct((db, t, d), F32),
                   jax.ShapeDtypeStruct((db, n_prev, d), F32)],
        grid=(db // sb,),
        in_specs=[pl.BlockSpec((sb, n_prev, d), blk),
                  pl.BlockSpec((sb, t, d), blk),
                  pl.BlockSpec((t, rows, d), lambda i: (0, 0, 0)),
                  pl.BlockSpec((1, d), fixed),
                  pl.BlockSpec((1, d), fixed),
                  pl.BlockSpec((1, d), fixed)],
        out_specs=[pl.BlockSpec((sb, t, d), blk), pl.BlockSpec((sb, n_prev, d), blk)],
        scratch_shapes=[pltpu.VMEM((rows, d), F32)],
        compiler_params=_cparams(1),
        name="conv_sample",
    )(state, u, w_shift, b_dw.reshape(1, -1), g_ln.reshape(1, -1), b_ln.reshape(1, -1))


TM_PROMPT = 1024


def _residual_add(accs, ex):
    out = ex[0] + accs[0]
    for e in ex[1:]:
        out = out + e
    return [out]


_mm = _matmul


def _normed(h, pair, g):
    return pair if pair is not None else (_rmsnorm(h, g, BF16), None)


def _row_extras(stat):
    return [] if stat is None else [(stat, "row")]


def _scaled(accs, ex, has_stat, which):
    if not has_stat:
        return accs, ex
    r = _row_rinv(ex[0], accs[0].shape[1])
    return [a * r if k in which else a for k, a in enumerate(accs)], ex[1:]


def _linear_residual(xs, xp, hs, hp, w, layer, name, g_next, bias=None):
    ms = xs.shape[0]
    cols = [] if bias is None else [(bias.reshape(1, -1), "col")]
    (hs2,), ns, (wb,) = _mm([xs], [(0, w, layer, 0)], [(hs, "tile")] + cols, _residual_add, [F32],
                            n_cols=D_MODEL, tm=ms, tn=256, name=name, emit_w=True, norm_gain=g_next)
    (hp2,), np_, _ = _mm([xp], [(0, wb, 0, 0)], [(hp, "tile")] + cols, _residual_add, [F32],
                         n_cols=D_MODEL, tm=TM_PROMPT, tn=512, name=name, norm_gain=g_next)
    return hs2, ns, hp2, np_


def _ffn(hs, ns, hp, np_, g, wg, wu, wd, layer, g_next):
    ms = hs.shape[0]
    d_ff = wg.shape[2]
    (zs, ss), (zp, sp) = _normed(hs, ns, g), _normed(hp, np_, g)

    def gate_up(has_stat):
        def epilogue(accs, ex):
            (a, b), _ = _scaled(accs, ex, has_stat, (0, 1))
            return [jax.nn.silu(a) * b]
        return epilogue

    (act_s,), _, (wgb, wub) = _mm([zs], [(0, wg, layer, 0), (0, wu, layer, 0)], _row_extras(ss),
                                  gate_up(ss is not None), [BF16],
                                  n_cols=d_ff, tm=ms, tn=256, name="ffn_gate_up", emit_w=True)
    (act_p,), _, _ = _mm([zp], [(0, wgb, 0, 0), (0, wub, 0, 0)], _row_extras(sp),
                         gate_up(sp is not None), [BF16],
                         n_cols=d_ff, tm=TM_PROMPT, tn=256, name="ffn_gate_up")
    (hs2,), ns2, (wdb,) = _mm([act_s], [(0, wd, layer, 0)], [(hs, "tile")], _residual_add, [F32],
                              n_cols=D_MODEL, tm=ms, tn=128, name="ffn_down", emit_w=True,
                              norm_gain=g_next)
    (hp2,), np2, _ = _mm([act_p], [(0, wdb, 0, 0)], [(hp, "tile")], _residual_add, [F32],
                         n_cols=D_MODEL, tm=512, tn=512, name="ffn_down", norm_gain=g_next)
    return hs2, ns2, hp2, np2


def _ple(hs, ns, hp, np_, ps, pp, g, w_gate, w_up, layer, g_next):
    ms = hs.shape[0]
    (zs, ss), (zp, sp) = _normed(hs, ns, g), _normed(hp, np_, g)

    def gated(has_stat):
        def epilogue(accs, ex):
            (a, b), ex = _scaled(accs, ex, has_stat, (0,))
            return [ex[0] + jax.nn.sigmoid(a) * b]
        return epilogue

    (hs2,), ns2, (wgb, wub) = _mm([zs, ps], [(0, w_gate, layer, 0), (1, w_up, layer, 0)],
                                  _row_extras(ss) + [(hs, "tile")], gated(ss is not None), [F32],
                                  n_cols=D_MODEL, tm=ms, tn=256, name="ple", emit_w=True,
                                  norm_gain=g_next)
    (hp2,), np2, _ = _mm([zp, pp], [(0, wgb, 0, 0), (1, wub, 0, 0)],
                         _row_extras(sp) + [(hp, "tile")], gated(sp is not None), [F32],
                         n_cols=D_MODEL, tm=TM_PROMPT, tn=512, name="ple", norm_gain=g_next)
    return hs2, ns2, hp2, np2


def _rope_tables(pos):
    half = ROPE_DIM // 2
    inv_freq = ROPE_THETA ** (-jnp.arange(half, dtype=F32) / half)
    ang = pos.astype(F32)[:, None] * inv_freq
    cos, sin = jnp.cos(ang), jnp.sin(ang)
    pad = jnp.zeros((pos.shape[0], ROPE_PAD - ROPE_DIM), F32)
    return (jnp.concatenate([cos, cos, pad], axis=1),
            jnp.concatenate([-sin, sin, pad], axis=1))


def _swap_halves(w):
    half = ROPE_DIM // 2
    return jnp.concatenate([w[..., half:], w[..., :half]], axis=-1)


def _pad_rope(w):
    return jnp.concatenate([w, jnp.zeros(w.shape[:-1] + (ROPE_PAD - ROPE_DIM,), w.dtype)], axis=-1)


def kernel(x_prompt, x_sample, cache_ckv, cache_kpe, state_conv, page_table, p_prompt, p_sample,
           g_attn_norm, w_dq, g_q, w_uq, w_dkv, g_kv, w_uk, w_uv, w_o,
           g_conv_norm, w_pw1, b_pw1, w_dw, b_dw, g_conv_ln, b_conv_ln, w_pw2, b_pw2,
           g_ffn_norm, w_ffn_gate, w_ffn_up, w_ffn_down,
           g_ple_norm, w_ple_gate, w_ple_up, g_final):
    B, S, _ = x_prompt.shape
    DB, T, _ = x_sample.shape
    depth = g_ffn_norm.shape[0]
    past = page_table.shape[1] * cache_ckv.shape[2]
    MP, MS = B * S, DB * T

    cos_p, sin_p = _rope_tables(jnp.tile(jnp.arange(S, dtype=jnp.int32), B))
    cos_s, sin_s = _rope_tables(jnp.tile(past + jnp.arange(T, dtype=jnp.int32), DB))
    tab_p = jnp.concatenate([cos_p[:, :ROPE_DIM], sin_p[:, :ROPE_DIM]], axis=1)
    tab_s = jnp.concatenate([cos_s[:, :ROPE_DIM], sin_s[:, :ROPE_DIM]], axis=1)

    hp = x_prompt.reshape(MP, D_MODEL)
    hs = x_sample.reshape(MS, D_MODEL)
    ckv_p, kpe_p, conv_p, ckv_s, kpe_s, conv_s = [], [], [], [], [], []
    ns = np_ = None

    for i in range(depth):
        j = i // 2
        if i % 2 == 0:
            wdq = w_dq[j].astype(BF16)
            w_kpe = w_dkv[j][:, KV_LORA:]
            wkv = jnp.concatenate([w_dkv[j][:, :KV_LORA], _pad_rope(w_kpe),
                                   _pad_rope(_swap_halves(w_kpe))], axis=1).astype(BF16)
            wq3 = w_uq[j].reshape(Q_LORA, N_HEADS, QK_DIM)
            wn = wq3[:, :, :NOPE_DIM].reshape(Q_LORA, N_HEADS * NOPE_DIM).astype(BF16)
            w_pe = wq3[:, :, NOPE_DIM:]
            wp = jnp.concatenate([w_pe, _swap_halves(w_pe)], axis=-1).reshape(
                Q_LORA, N_HEADS * ROPE_PAD).astype(BF16)
            wuk_t = jnp.transpose(w_uk[j], (1, 2, 0)).astype(BF16)
            wuv_t = jnp.transpose(w_uv[j], (1, 0, 2)).astype(BF16)

            z = _rmsnorm(hs, g_attn_norm[j], BF16)
            cq, ck, kp, kk = _kv_proj(z, wdq, wkv, g_q[j], g_kv[j], cos_s, sin_s)
            q = _q_proj(cq, wn, wp, wuk_t, tab_s, tm=MS)
            knew = jnp.pad(kk.reshape(DB, T, QK_PAD), ((0, 0), (0, DEC_NEW_PAD - T), (0, 0)))
            o_lat = _decode_attn(q.reshape(DB, T * N_HEADS, QK_PAD), knew,
                                 cache_ckv, jnp.swapaxes(cache_kpe, 2, 3), page_table, j, T)
            o_s = _uv_proj(o_lat.reshape(MS, N_HEADS * KV_LORA), wuv_t, tm=MS)
            ckv_s.append(ck.reshape(DB, T, KV_LORA))
            kpe_s.append(kp.reshape(DB, T, ROPE_DIM))

            z = _rmsnorm(hp, g_attn_norm[j], BF16)
            cq, ck, kp, kk = _kv_proj(z, wdq, wkv, g_q[j], g_kv[j], cos_p, sin_p)
            q = _q_proj(cq, wn, wp, wuk_t, tab_p, tm=TM_PROMPT)
            o_p = _prompt_attn(q, kk, wuv_t, B, S)
            ckv_p.append(ck.reshape(B, S, KV_LORA))
            kpe_p.append(kp.reshape(B, S, ROPE_DIM))

            hs, ns, hp, np_ = _linear_residual(o_s, o_p, hs, hp, w_o, j, "attn_out", g_ffn_norm[i])
        else:
            b1 = b_pw1[j].reshape(1, -1)
            biases = [(b1[:, :D_MODEL], "col"), (b1[:, D_MODEL:], "col")]
            (zs, ss), (zp, sp) = _normed(hs, ns, g_conv_norm[j]), _normed(hp, np_, g_conv_norm[j])

            def glu(has_stat):
                def epilogue(accs, ex):
                    (a, b), ex = _scaled(accs, ex, has_stat, (0, 1))
                    return [(a + ex[0]) * jax.nn.sigmoid(b + ex[1])]
                return epilogue

            (u_s,), _, (w1a, w1b) = _mm([zs], [(0, w_pw1, j, 0), (0, w_pw1, j, D_MODEL // 256)],
                                        _row_extras(ss) + biases, glu(ss is not None), [F32],
                                        n_cols=D_MODEL, tm=MS, tn=256, name="pw1_glu", emit_w=True)
            (u_p,), _, _ = _mm([zp], [(0, w1a, 0, 0), (0, w1b, 0, 0)], _row_extras(sp) + biases,
                               glu(sp is not None), [F32],
                               n_cols=D_MODEL, tm=TM_PROMPT, tn=512, name="pw1_glu")
            c_s, new_state = _conv_sample(state_conv[j], u_s.reshape(DB, T, D_MODEL),
                                          w_dw[j], b_dw[j], g_conv_ln[j], b_conv_ln[j])
            c_p = _conv_prompt(u_p, w_dw[j], b_dw[j], g_conv_ln[j], b_conv_ln[j], B, S)
            conv_s.append(new_state)
            conv_p.append(u_p.reshape(B, S, D_MODEL)[:, S - (CONV_W - 1):])
            hs, ns, hp, np_ = _linear_residual(c_s.reshape(MS, D_MODEL), c_p, hs, hp, w_pw2, j, "pw2",
                                               g_ffn_norm[i], bias=b_pw2[j])

        hs, ns, hp, np_ = _ffn(hs, ns, hp, np_, g_ffn_norm[i], w_ffn_gate, w_ffn_up, w_ffn_down, i,
                               g_ple_norm[i])
        next_is_conv = i + 1 < depth and (i + 1) % 2 == 1
        hs, ns, hp, np_ = _ple(hs, ns, hp, np_, p_sample[i].reshape(MS, -1), p_prompt[i].reshape(MP, -1),
                               g_ple_norm[i], w_ple_gate, w_ple_up, i,
                               g_conv_norm[(i + 1) // 2] if next_is_conv else None)

    y_prompt = _rmsnorm(hp, g_final, F32).reshape(B, S, D_MODEL)
    y_sample = _rmsnorm(hs, g_final, F32).reshape(DB, T, D_MODEL)
    return (y_prompt, y_sample,
            jnp.stack(ckv_p), jnp.stack(kpe_p), jnp.stack(conv_p),
            jnp.stack(ckv_s), jnp.stack(kpe_s), jnp.stack(conv_s))
```
